```python
import jax, jax.numpy as jnp
from jax import lax
import numpy as np

D_MODEL = 2048
BATCH = 2
SEQ = 16384
DEPTH = 2

N_MIXERS = 2
CHUNK = 128
EPS = 1e-6
ROPE_BASE = 10000.0
RET_HEADS = 8
RET_QK_DIM = D_MODEL
RET_V_DIM = 2 * D_MODEL
RET_QK_HEAD = RET_QK_DIM // RET_HEADS
RET_V_HEAD = RET_V_DIM // RET_HEADS
RET_IN_DIM = 2 * RET_QK_DIM + 2 * RET_V_DIM
ML_INNER = 2 * D_MODEL
ML_HEADS = 4
ML_HEAD = ML_INNER // ML_HEADS
QKV_BLOCK = 4
N_QKV_BLOCKS = ML_INNER // QKV_BLOCK
CONV_WIDTH = 4
D_FF = -(-8 * D_MODEL // (3 * 256)) * 256
N_RET_LAYERS = (DEPTH + 1) // 2
N_ML_LAYERS = DEPTH // 2

kernel_name = 'retention_mlstm_interleaved_trunk'


def rms_norm(x, g):
    xf = x.astype(jnp.float32)
    y = xf * lax.rsqrt(jnp.mean(xf * xf, axis=-1, keepdims=True) + EPS)
    return (y * g.astype(jnp.float32)).astype(x.dtype)


def head_layer_norm(h, g):
    mu = jnp.mean(h, axis=-1, keepdims=True)
    var = jnp.mean(jnp.square(h - mu), axis=-1, keepdims=True)
    y = (h - mu) * lax.rsqrt(var + EPS)
    return y.reshape(h.shape[0], h.shape[1], -1) * g.astype(jnp.float32)


def to_chunks(t):
    b, s, h, d = t.shape
    return t.reshape(b, s // CHUNK, CHUNK, h, d).transpose(1, 0, 3, 2, 4)


def from_chunks(t):
    n, b, h, l, d = t.shape
    return t.transpose(1, 0, 3, 2, 4).reshape(b, n * l, h, d)


def rotary(t, cos, sin):
    half = t.shape[-1] // 2
    t1, t2 = t[..., :half], t[..., half:]
    return jnp.concatenate([t1 * cos - t2 * sin, t1 * sin + t2 * cos], axis=-1)


def retention_mixer(u, positions, w_in, gn_g, w_out):
    b, s, _ = u.shape
    f32 = jnp.float32
    q, k, v, g = jnp.split(u @ w_in, [RET_QK_DIM, 2 * RET_QK_DIM, 2 * RET_QK_DIM + RET_V_DIM], axis=-1)
    q = q.astype(f32).reshape(b, s, RET_HEADS, RET_QK_HEAD)
    k = k.astype(f32).reshape(b, s, RET_HEADS, RET_QK_HEAD)
    v = v.astype(f32).reshape(b, s, RET_HEADS, RET_V_HEAD)
    inv_freq = ROPE_BASE ** (-jnp.arange(RET_QK_HEAD // 2, dtype=f32) * (2.0 / RET_QK_HEAD))
    ang = positions.astype(f32)[..., None] * inv_freq
    cos = jnp.cos(ang)[:, :, None, :]
    sin = jnp.sin(ang)[:, :, None, :]
    q = rotary(q, cos, sin)
    k = rotary(k, cos, sin) * (RET_QK_HEAD ** -0.5)
    log_gamma = jnp.log(1.0 - 2.0 ** (-5.0 - jnp.arange(RET_HEADS, dtype=f32)))
    idx = jnp.arange(CHUNK, dtype=f32)
    rel = idx[:, None] - idx[None, :]
    d_intra = jnp.where(rel[None] >= 0, jnp.exp(jnp.maximum(rel, 0.0)[None] * log_gamma[:, None, None]), 0.0)
    xi = jnp.exp((idx + 1.0)[None] * log_gamma[:, None])
    zeta = jnp.exp((CHUNK - 1.0 - idx)[None] * log_gamma[:, None])
    chunk_decay = jnp.exp(CHUNK * log_gamma)

    def step(state, inp):
        qc, kc, vc = inp
        scores = jnp.einsum('bhid,bhjd->bhij', qc, kc) * d_intra
        out = (jnp.einsum('bhij,bhje->bhie', scores, vc)
               + jnp.einsum('bhid,bhde->bhie', qc * xi[None, :, :, None], state))
        state = (chunk_decay[None, :, None, None] * state
                 + jnp.einsum('bhjd,bhje->bhde', kc * zeta[None, :, :, None], vc))
        return state, out

    state0 = jnp.zeros((b, RET_HEADS, RET_QK_HEAD, RET_V_HEAD), f32)
    _, o = lax.scan(step, state0, (to_chunks(q), to_chunks(k), to_chunks(v)))
    y = head_layer_norm(from_chunks(o), gn_g)
    y = jax.nn.silu(g.astype(f32)) * y
    return y.astype(u.dtype) @ w_out


def mlstm_mixer(u, w_in, conv_w, conv_b, w_q, w_k, w_v, w_gate, b_gate, gn_g, skip, w_out):
    b, s, _ = u.shape
    f32 = jnp.float32
    xm, z = jnp.split(u @ w_in, 2, axis=-1)
    xp = jnp.pad(xm, ((0, 0), (CONV_WIDTH - 1, 0), (0, 0)))
    conv = conv_b + sum(xp[:, j:j + s, :] * conv_w[j] for j in range(CONV_WIDTH))
    xc = jax.nn.silu(conv)

    def block_diag(t, w):
        tb = t.reshape(b, s, N_QKV_BLOCKS, QKV_BLOCK)
        return jnp.einsum('bsni,noi->bsno', tb, w).reshape(b, s, ML_INNER)

    q = block_diag(xc, w_q)
    k = block_diag(xc, w_k)
    v = block_diag(xm, w_v)
    gates = (q @ w_gate[:ML_INNER] + k @ w_gate[ML_INNER:2 * ML_INNER]
             + v @ w_gate[2 * ML_INNER:] + b_gate).astype(f32)
    ig = gates[..., :ML_HEADS]
    log_f = jax.nn.log_sigmoid(gates[..., ML_HEADS:])

    def gate_chunks(t):
        return t.reshape(b, s // CHUNK, CHUNK, ML_HEADS).transpose(1, 0, 3, 2)

    ig_c = gate_chunks(ig)
    cum_f = jnp.cumsum(gate_chunks(log_f), axis=-1)
    qh = q.astype(f32).reshape(b, s, ML_HEADS, ML_HEAD)
    kh = k.astype(f32).reshape(b, s, ML_HEADS, ML_HEAD) * (ML_HEAD ** -0.5)
    vh = v.astype(f32).reshape(b, s, ML_HEADS, ML_HEAD)
    causal = jnp.tril(jnp.ones((CHUNK, CHUNK), dtype=bool))

    def step(carry, inp):
        c_mat, n_vec, m = carry
        qc, kc, vc, igc, bc = inp
        log_d = jnp.where(causal, bc[..., :, None] - bc[..., None, :] + igc[..., None, :], -jnp.inf)
        m_inter = bc + m[..., None]
        m_t = jnp.maximum(jnp.max(log_d, axis=-1), m_inter)
        scores = jnp.einsum('bhtd,bhsd->bhts', qc, kc) * jnp.exp(log_d - m_t[..., None])
        inter = jnp.exp(m_inter - m_t)
        num = (jnp.einsum('bhts,bhse->bhte', scores, vc)
               + inter[..., None] * jnp.einsum('bhtd,bhde->bhte', qc, c_mat))
        den = jnp.sum(scores, axis=-1) + inter * jnp.einsum('bhtd,bhd->bht', qc, n_vec)
        h = num / jnp.maximum(jnp.abs(den), jnp.exp(-m_t))[..., None]
        b_last = bc[..., -1]
        log_w = b_last[..., None] - bc + igc
        m_new = jnp.maximum(b_last + m, jnp.max(log_w, axis=-1))
        wts = jnp.exp(log_w - m_new[..., None])
        decay = jnp.exp(b_last + m - m_new)
        kw = kc * wts[..., None]
        c_mat = decay[..., None, None] * c_mat + jnp.einsum('bhsd,bhse->bhde', kw, vc)
        n_vec = decay[..., None] * n_vec + jnp.sum(kw, axis=-2)
        return (c_mat, n_vec, m_new), h

    carry0 = (jnp.zeros((b, ML_HEADS, ML_HEAD, ML_HEAD), f32),
              jnp.zeros((b, ML_HEADS, ML_HEAD), f32),
              jnp.zeros((b, ML_HEADS), f32))
    _, hc = lax.scan(step, carry0, (to_chunks(qh), to_chunks(kh), to_chunks(vh), ig_c, cum_f))
    hn = head_layer_norm(from_chunks(hc), gn_g)
    hs = (hn + skip.astype(f32) * xc.astype(f32)) * jax.nn.silu(z.astype(f32))
    return hs.astype(u.dtype) @ w_out


def swiglu(u, w_gate, w_up, w_down):
    return (jax.nn.silu(u @ w_gate) * (u @ w_up)) @ w_down


def setup_inputs(seed: int = 0) -> dict:
    key = jax.random.key(seed)
    ks = jax.random.split(key, 24)

    def nrm(k, shape, scale):
        return jax.random.normal(k, shape, jnp.float32) * scale

    x = jax.random.normal(ks[0], (BATCH, SEQ, D_MODEL), jnp.float32)
    positions = jnp.broadcast_to(jnp.arange(SEQ, dtype=jnp.int32), (BATCH, SEQ))
    norm_mix_g = 1.0 + nrm(ks[1], (DEPTH, D_MODEL), 0.02)
    norm_ffn_g = 1.0 + nrm(ks[2], (DEPTH, D_MODEL), 0.02)
    ret_w_in = nrm(ks[3], (N_RET_LAYERS, D_MODEL, RET_IN_DIM), D_MODEL ** -0.5)
    ret_gn_g = 1.0 + nrm(ks[4], (N_RET_LAYERS, RET_V_DIM), 0.02)
    ret_w_out = nrm(ks[5], (N_RET_LAYERS, RET_V_DIM, D_MODEL), RET_V_DIM ** -0.5)
    ml_w_in = nrm(ks[6], (N_ML_LAYERS, D_MODEL, 2 * ML_INNER), D_MODEL ** -0.5)
    ml_conv_w = nrm(ks[7], (N_ML_LAYERS, CONV_WIDTH, ML_INNER), CONV_WIDTH ** -0.5)
    ml_conv_b = nrm(ks[8], (N_ML_LAYERS, ML_INNER), 0.01)
    ml_w_q = nrm(ks[9], (N_ML_LAYERS, N_QKV_BLOCKS, QKV_BLOCK, QKV_BLOCK), QKV_BLOCK ** -0.5)
    ml_w_k = nrm(ks[10], (N_ML_LAYERS, N_QKV_BLOCKS, QKV_BLOCK, QKV_BLOCK), QKV_BLOCK ** -0.5)
    ml_w_v = nrm(ks[11], (N_ML_LAYERS, N_QKV_BLOCKS, QKV_BLOCK, QKV_BLOCK), QKV_BLOCK ** -0.5)
    ml_w_gate = nrm(ks[12], (N_ML_LAYERS, 3 * ML_INNER, 2 * ML_HEADS), (3 * ML_INNER) ** -0.5)
    ig_bias = nrm(ks[13], (N_ML_LAYERS, ML_HEADS), 0.1)
    fg_bias = jnp.linspace(3.0, 6.0, ML_HEADS, dtype=jnp.float32)[None] + nrm(ks[14], (N_ML_LAYERS, ML_HEADS), 0.01)
    ml_b_gate = jnp.concatenate([ig_bias, fg_bias], axis=-1)
    ml_gn_g = 1.0 + nrm(ks[15], (N_ML_LAYERS, ML_INNER), 0.02)
    ml_skip = 1.0 + nrm(ks[16], (N_ML_LAYERS, ML_INNER), 0.02)
    ml_w_out = nrm(ks[17], (N_ML_LAYERS, ML_INNER, D_MODEL), ML_INNER ** -0.5)
    ffn_w_gate = nrm(ks[18], (DEPTH, D_MODEL, D_FF), D_MODEL ** -0.5)
    ffn_w_up = nrm(ks[19], (DEPTH, D_MODEL, D_FF), D_MODEL ** -0.5)
    ffn_w_down = nrm(ks[20], (DEPTH, D_FF, D_MODEL), D_FF ** -0.5)
    final_g = 1.0 + nrm(ks[21], (D_MODEL,), 0.02)
    return {'x': x, 'positions': positions, 'norm_mix_g': norm_mix_g, 'norm_ffn_g': norm_ffn_g,
            'ret_w_in': ret_w_in, 'ret_gn_g': ret_gn_g, 'ret_w_out': ret_w_out,
            'ml_w_in': ml_w_in, 'ml_conv_w': ml_conv_w, 'ml_conv_b': ml_conv_b,
            'ml_w_q': ml_w_q, 'ml_w_k': ml_w_k, 'ml_w_v': ml_w_v,
            'ml_w_gate': ml_w_gate, 'ml_b_gate': ml_b_gate, 'ml_gn_g': ml_gn_g,
            'ml_skip': ml_skip, 'ml_w_out': ml_w_out,
            'ffn_w_gate': ffn_w_gate, 'ffn_w_up': ffn_w_up, 'ffn_w_down': ffn_w_down,
            'final_g': final_g}


def reference(x, positions, norm_mix_g, norm_ffn_g, ret_w_in, ret_gn_g, ret_w_out,
              ml_w_in, ml_conv_w, ml_conv_b, ml_w_q, ml_w_k, ml_w_v, ml_w_gate, ml_b_gate,
              ml_gn_g, ml_skip, ml_w_out, ffn_w_gate, ffn_w_up, ffn_w_down, final_g):
    h = x
    for i in range(DEPTH):
        j = i // N_MIXERS
        u = rms_norm(h, norm_mix_g[i])
        if i % N_MIXERS == 0:
            h = h + retention_mixer(u, positions, ret_w_in[j], ret_gn_g[j], ret_w_out[j])
        else:
            h = h + mlstm_mixer(u, ml_w_in[j], ml_conv_w[j], ml_conv_b[j], ml_w_q[j], ml_w_k[j],
                                ml_w_v[j], ml_w_gate[j], ml_b_gate[j], ml_gn_g[j], ml_skip[j],
                                ml_w_out[j])
        u = rms_norm(h, norm_ffn_g[i])
        h = h + swiglu(u, ffn_w_gate[i], ffn_w_up[i], ffn_w_down[i])
    return rms_norm(h, final_g)
```

```python
import functools

import jax
import jax.numpy as jnp
from jax import lax
from jax.experimental import pallas as pl
from jax.experimental.pallas import tpu as pltpu

F32 = jnp.float32
BF16 = jnp.bfloat16

EPS = 1e-6
ROPE_BASE = 10000.0
RET_HEADS = 8
ML_HEADS = 4
QKV_BLOCK = 4
CONV_WIDTH = 4

V7X_VMEM_BYTES = 64 * 1024 * 1024
V7X_LANES = 128
V7X_SUBLANES = 8
V7X_MXU_DIM = 256

CHUNK_ROWS = V7X_MXU_DIM
STEP_ROWS = 512
VMEM_LIMIT = V7X_VMEM_BYTES - 8 * 1024 * 1024


def _tile(n, pref):
    t = min(pref, n)
    while n % t:
        t //= 2
    return t


def _params(*semantics):
    return pltpu.CompilerParams(dimension_semantics=semantics, vmem_limit_bytes=VMEM_LIMIT)


def _sigmoid(x):
    return 1.0 / (1.0 + jnp.exp(-x))


def _dot(a, b):
    return jnp.dot(a, b, preferred_element_type=F32)


def _dot_nt(a, b):
    return lax.dot_general(a, b, (((1,), (1,)), ((), ())), preferred_element_type=F32)


def _dot_tn(a, b):
    return lax.dot_general(a, b, (((0,), (0,)), ((), ())), preferred_element_type=F32)


def _rms(x, g):
    ms = jnp.mean(x * x, axis=-1, keepdims=True)
    return x * lax.rsqrt(ms + EPS) * g


def _head_norm(o, g):
    mu = jnp.mean(o, axis=-1, keepdims=True)
    d = o - mu
    var = jnp.mean(d * d, axis=-1, keepdims=True)
    return d * lax.rsqrt(var + EPS) * g


def _rmsnorm_kernel(x_ref, g_ref, o_ref):
    o_ref[...] = _rms(x_ref[...], g_ref[...]).astype(o_ref.dtype)


def _rmsnorm(x, g):
    t, d = x.shape
    tm = _tile(t, 512)
    return pl.pallas_call(
        _rmsnorm_kernel,
        grid=(t // tm,),
        in_specs=[pl.BlockSpec((tm, d), lambda i: (i, 0)),
                  pl.BlockSpec((1, d), lambda i: (0, 0))],
        out_specs=pl.BlockSpec((tm, d), lambda i: (i, 0)),
        out_shape=jax.ShapeDtypeStruct((t, d), BF16),
        compiler_params=_params("parallel"),
        name="rmsnorm",
    )(x, g.reshape(1, d))


def _rope_kernel(pos_ref, invf_ref, cos_ref, sin_ref):
    ang = pos_ref[...].astype(F32) * invf_ref[...]
    cos_ref[...] = jnp.cos(ang)
    sin_ref[...] = jnp.sin(ang)


def _rope_table(pos, inv_freq):
    t = pos.shape[0]
    half = inv_freq.shape[0]
    tm = _tile(t, 2048)
    return pl.pallas_call(
        _rope_kernel,
        grid=(t // tm,),
        in_specs=[pl.BlockSpec((tm, 1), lambda i: (i, 0)),
                  pl.BlockSpec((1, half), lambda i: (0, 0))],
        out_specs=[pl.BlockSpec((tm, half), lambda i: (i, 0))] * 2,
        out_shape=[jax.ShapeDtypeStruct((t, half), F32)] * 2,
        compiler_params=_params("parallel"),
        name="rope_table",
    )(pos.reshape(t, 1), inv_freq.reshape(1, half))


def _ret_in_kernel(u_ref, w_ref, cos_ref, sin_ref, o_ref, *, tn, dk, n_q_blocks, k_scale):
    j = pl.program_id(1)
    acc = _dot(u_ref[...], w_ref[...])

    @pl.when(j < 2 * n_q_blocks)
    def _():
        c = cos_ref[...]
        s = sin_ref[...]
        scale = jnp.where(j >= n_q_blocks, k_scale, 1.0).astype(F32)
        half = dk // 2
        for hh in range(tn // dk):
            t1 = acc[:, hh * dk:hh * dk + half]
            t2 = acc[:, hh * dk + half:(hh + 1) * dk]
            o_ref[:, hh * dk:hh * dk + half] = ((t1 * c - t2 * s) * scale).astype(o_ref.dtype)
            o_ref[:, hh * dk + half:(hh + 1) * dk] = ((t1 * s + t2 * c) * scale).astype(o_ref.dtype)

    @pl.when(j >= 2 * n_q_blocks)
    def _():
        o_ref[...] = acc.astype(o_ref.dtype)


def _ret_in(u, w, cos, sin, *, dk, qk_dim):
    t, d = u.shape
    n = w.shape[1]
    tm = _tile(t, 1024)
    tn = 512
    kern = functools.partial(_ret_in_kernel, tn=tn, dk=dk, n_q_blocks=qk_dim // tn,
                             k_scale=float(dk) ** -0.5)
    return pl.pallas_call(
        kern,
        grid=(t // tm, n // tn),
        in_specs=[pl.BlockSpec((tm, d), lambda i, j: (i, 0)),
                  pl.BlockSpec((d, tn), lambda i, j: (0, j)),
                  pl.BlockSpec((tm, dk // 2), lambda i, j: (i, 0)),
                  pl.BlockSpec((tm, dk // 2), lambda i, j: (i, 0))],
        out_specs=pl.BlockSpec((tm, tn), lambda i, j: (i, j)),
        out_shape=jax.ShapeDtypeStruct((t, n), BF16),
        compiler_params=_params("parallel", "arbitrary"),
        name="ret_in",
    )(u, w, cos, sin)


def _ret_chunk_kernel(lg_ref, q_ref, k_ref, v_ref, g_ref, gn_ref, o_ref, state_ref, *, rows, n_sub):
    h = pl.program_id(1)

    @pl.when(pl.program_id(2) == 0)
    def _():
        state_ref[...] = jnp.zeros_like(state_ref)

    lg = lg_ref[h]
    ri = lax.broadcasted_iota(jnp.int32, (rows, rows), 0)
    ci = lax.broadcasted_iota(jnp.int32, (rows, rows), 1)
    rel = (ri - ci).astype(F32)
    dmat = jnp.where(rel >= 0.0, jnp.exp(jnp.maximum(rel, 0.0) * lg), 0.0)
    idx = lax.broadcasted_iota(jnp.int32, (rows, 1), 0).astype(F32)
    xi = jnp.exp((idx + 1.0) * lg)
    zeta = jnp.exp((rows - 1.0 - idx) * lg)
    chunk_decay = jnp.exp(jnp.full((1, 1), float(rows), F32) * lg)
    gn = gn_ref[...]

    for c in range(n_sub):
        sl = slice(c * rows, (c + 1) * rows)
        q = q_ref[sl, :]
        k = k_ref[sl, :]
        v = v_ref[sl, :]
        state = state_ref[...]
        scores = _dot_nt(q, k) * dmat
        o = _dot(scores.astype(BF16), v) + _dot(q, state.astype(BF16)) * xi
        kz = (k.astype(F32) * zeta).astype(BF16)
        state_ref[...] = chunk_decay * state + _dot_tn(kz, v)
        y = _head_norm(o, gn)
        g = g_ref[sl, :].astype(F32)
        o_ref[sl, :] = (y * (g * _sigmoid(g))).astype(o_ref.dtype)


def _ret_chunk(qkvg, gn_g, log_gamma, *, batch, seq, dk, dv, qk_dim, v_dim):
    t = qkvg.shape[0]
    heads = qk_dim // dk
    step = _tile(seq, STEP_ROWS)
    rows = _tile(step, CHUNK_ROWS)
    nsteps = seq // step
    row = lambda b, h, n: b * nsteps + n
    kern = functools.partial(_ret_chunk_kernel, rows=rows, n_sub=step // rows)
    return pl.pallas_call(
        kern,
        grid=(batch, heads, nsteps),
        in_specs=[pl.BlockSpec(memory_space=pltpu.SMEM),
                  pl.BlockSpec((step, dk), lambda b, h, n: (row(b, h, n), h)),
                  pl.BlockSpec((step, dk), lambda b, h, n: (row(b, h, n), qk_dim // dk + h)),
                  pl.BlockSpec((step, dv), lambda b, h, n: (row(b, h, n), 2 * qk_dim // dv + h)),
                  pl.BlockSpec((step, dv), lambda b, h, n: (row(b, h, n), (2 * qk_dim + v_dim) // dv + h)),
                  pl.BlockSpec((1, dv), lambda b, h, n: (0, h))],
        out_specs=pl.BlockSpec((step, dv), lambda b, h, n: (row(b, h, n), h)),
        out_shape=jax.ShapeDtypeStruct((t, v_dim), BF16),
        scratch_shapes=[pltpu.VMEM((dk, dv), F32)],
        compiler_params=_params("parallel", "parallel", "arbitrary"),
        name="ret_chunk",
    )(log_gamma, qkvg, qkvg, qkvg, qkvg, gn_g.reshape(1, v_dim))


def _proj_residual_kernel(y_ref, w_ref, h_ref, g_ref, *out_refs, final):
    hn = h_ref[...] + _dot(y_ref[...], w_ref[...])
    if final:
        out_refs[0][...] = _rms(hn, g_ref[...])
    else:
        out_refs[0][...] = hn
        out_refs[1][...] = _rms(hn, g_ref[...]).astype(out_refs[1].dtype)


def _proj_residual(y, w, h, g, *, final=False):
    t, k = y.shape
    d = w.shape[1]
    tm = _tile(t, 256)
    row_spec = pl.BlockSpec((tm, d), lambda i: (i, 0))
    if final:
        out_specs, out_shape = row_spec, jax.ShapeDtypeStruct((t, d), F32)
    else:
        out_specs = [row_spec, row_spec]
        out_shape = [jax.ShapeDtypeStruct((t, d), F32), jax.ShapeDtypeStruct((t, d), BF16)]
    return pl.pallas_call(
        functools.partial(_proj_residual_kernel, final=final),
        grid=(t // tm,),
        in_specs=[pl.BlockSpec((tm, k), lambda i: (i, 0)),
                  pl.BlockSpec((k, d), lambda i: (0, 0), pipeline_mode=pl.Buffered(1)),
                  row_spec,
                  pl.BlockSpec((1, d), lambda i: (0, 0))],
        out_specs=out_specs,
        out_shape=out_shape,
        compiler_params=_params("parallel"),
        name="proj_residual_final" if final else "proj_residual",
    )(y, w, h, g.reshape(1, d))


def _ffn_up_kernel(u_ref, wg_ref, wu_ref, o_ref):
    u = u_ref[...]
    a = _dot(u, wg_ref[...])
    b = _dot(u, wu_ref[...])
    o_ref[...] = (a * _sigmoid(a) * b).astype(o_ref.dtype)


def _ffn_up(u, wg, wu):
    t, d = u.shape
    f = wg.shape[1]
    tm = _tile(t, 1024)
    tf = _tile(f, 512)
    return pl.pallas_call(
        _ffn_up_kernel,
        grid=(t // tm, f // tf),
        in_specs=[pl.BlockSpec((tm, d), lambda i, j: (i, 0)),
                  pl.BlockSpec((d, tf), lambda i, j: (0, j)),
                  pl.BlockSpec((d, tf), lambda i, j: (0, j))],
        out_specs=pl.BlockSpec((tm, tf), lambda i, j: (i, j)),
        out_shape=jax.ShapeDtypeStruct((t, f), BF16),
        compiler_params=_params("parallel", "arbitrary"),
        name="ffn_up",
    )(u, wg, wu)


def _ml_in_kernel(u_ref, wx_ref, wz_ref, cw_ref, cb_ref, bq_ref, bk_ref, bv_ref,
                  gq_ref, gk_ref, gv_ref, bg_ref,
                  q_ref, k_ref, v_ref, xc_ref, z_ref, gates_ref,
                  ext_ref, halo_ref, *, tm, tn, seq):
    i = pl.program_id(0)
    j = pl.program_id(1)
    u = u_ref[...]
    xm = _dot(u, wx_ref[...])
    z_ref[...] = _dot(u, wz_ref[...]).astype(z_ref.dtype)

    seq_start = (i * tm) % seq == 0
    ext_ref[0:V7X_SUBLANES, :] = jnp.where(seq_start, 0.0, halo_ref[j])
    ext_ref[V7X_SUBLANES:, :] = xm
    halo_ref[j] = xm[tm - V7X_SUBLANES:, :]
    conv = cb_ref[...]
    for tap in range(CONV_WIDTH):
        off = V7X_SUBLANES - (CONV_WIDTH - 1) + tap
        conv = conv + ext_ref[off:off + tm, :] * cw_ref[tap:tap + 1, :]
    xc = conv * _sigmoid(conv)
    xc_ref[...] = xc.astype(xc_ref.dtype)

    @pl.when(j == 0)
    def _():
        gates_ref[...] = jnp.broadcast_to(bg_ref[...], gates_ref.shape)

    gacc = jnp.zeros(gates_ref.shape, F32)
    for c in range(tn // V7X_MXU_DIM):
        sl = slice(c * V7X_MXU_DIM, (c + 1) * V7X_MXU_DIM)
        xcb = xc[:, sl].astype(BF16)
        xmb = xm[:, sl].astype(BF16)
        q = _dot(xcb, bq_ref[c]).astype(BF16)
        k = _dot(xcb, bk_ref[c]).astype(BF16)
        v = _dot(xmb, bv_ref[c]).astype(BF16)
        q_ref[:, sl] = q
        k_ref[:, sl] = k
        v_ref[:, sl] = v
        gacc = gacc + _dot(q, gq_ref[sl, :]) + _dot(k, gk_ref[sl, :]) + _dot(v, gv_ref[sl, :])
    gates_ref[...] += gacc


def _block_diag_tiles(w, tile):
    nb, bo, bi = w.shape
    per = tile // bi
    wt = jnp.swapaxes(w, 1, 2).reshape(nb // per, per, bi, bo)
    eye = jnp.eye(per, dtype=w.dtype)
    full = jnp.einsum("tbio,bc->tbico", wt, eye)
    return full.reshape(nb // per, tile, tile)


def _ml_in(u, w_in, conv_w, conv_b, w_q, w_k, w_v, w_gate, b_gate, *, seq):
    t, d = u.shape
    inner = w_in.shape[1] // 2
    tm = _tile(seq, 1024)
    tn = 512
    nj = inner // tn
    per_tile = tn // V7X_MXU_DIM
    ngates = w_gate.shape[1]

    def pad_gate(wg):
        return jnp.pad(wg, ((0, 0), (0, V7X_LANES - ngates))).astype(BF16)

    gq, gk, gv = (pad_gate(w_gate[s * inner:(s + 1) * inner]) for s in range(3))
    bg = jnp.pad(b_gate, (0, V7X_LANES - ngates)).reshape(1, V7X_LANES).astype(F32)
    bq, bk, bv = (_block_diag_tiles(w, V7X_MXU_DIM).astype(BF16) for w in (w_q, w_k, w_v))

    col = pl.BlockSpec((tm, tn), lambda i, j: (i, j))
    bd = pl.BlockSpec((per_tile, V7X_MXU_DIM, V7X_MXU_DIM), lambda i, j: (j, 0, 0))
    gw = pl.BlockSpec((tn, V7X_LANES), lambda i, j: (j, 0))
    act = jax.ShapeDtypeStruct((t, inner), BF16)
    kern = functools.partial(_ml_in_kernel, tm=tm, tn=tn, seq=seq)
    return pl.pallas_call(
        kern,
        grid=(t // tm, nj),
        in_specs=[pl.BlockSpec((tm, d), lambda i, j: (i, 0)),
                  pl.BlockSpec((d, tn), lambda i, j: (0, j)),
                  pl.BlockSpec((d, tn), lambda i, j: (0, nj + j)),
                  pl.BlockSpec((CONV_WIDTH, tn), lambda i, j: (0, j)),
                  pl.BlockSpec((1, tn), lambda i, j: (0, j)),
                  bd, bd, bd, gw, gw, gw,
                  pl.BlockSpec((1, V7X_LANES), lambda i, j: (0, 0))],
        out_specs=[col, col, col, col, col,
                   pl.BlockSpec((tm, V7X_LANES), lambda i, j: (i, 0))],
        out_shape=[act, act, act, act, act, jax.ShapeDtypeStruct((t, V7X_LANES), F32)],
        scratch_shapes=[pltpu.VMEM((tm + V7X_SUBLANES, tn), F32),
                        pltpu.VMEM((nj, V7X_SUBLANES, tn), F32)],
        compiler_params=_params("arbitrary", "arbitrary"),
        name="ml_in",
    )(u, w_in, w_in, conv_w, conv_b.reshape(1, inner), bq, bk, bv, gq, gk, gv, bg)


def _log_sigmoid(x):
    return jnp.minimum(x, 0.0) - jnp.log1p(jnp.exp(-jnp.abs(x)))


def _ml_chunk_kernel(q_ref, k_ref, v_ref, gates_ref, xc_ref, z_ref, gn_ref, skip_ref, o_ref,
                     c_ref, n_ref, m_ref, *, rows, n_sub, heads, k_scale):
    h = pl.program_id(1)

    @pl.when(pl.program_id(2) == 0)
    def _():
        c_ref[...] = jnp.zeros_like(c_ref)
        n_ref[...] = jnp.zeros_like(n_ref)
        m_ref[...] = jnp.zeros_like(m_ref)

    lanes = gates_ref.shape[1]
    ri = lax.broadcasted_iota(jnp.int32, (rows, rows), 0)
    ci = lax.broadcasted_iota(jnp.int32, (rows, rows), 1)
    causal = ri >= ci
    tri = causal.astype(F32)
    lane = lax.broadcasted_iota(jnp.int32, (1, lanes), 1)
    sub = lax.broadcasted_iota(jnp.int32, (lanes, 1), 0)
    gn = gn_ref[...]
    skip = skip_ref[...]

    for c in range(n_sub):
        sl = slice(c * rows, (c + 1) * rows)
        gates = gates_ref[sl, :]
        log_f = _log_sigmoid(gates)
        cum = jnp.dot(tri, log_f, precision=lax.Precision.HIGHEST,
                      preferred_element_type=F32)
        gates_t = gates.T
        cum_t = cum.T
        ig = jnp.sum(jnp.where(lane == h, gates, 0.0), axis=1, keepdims=True)
        bc = jnp.sum(jnp.where(lane == heads + h, cum, 0.0), axis=1, keepdims=True)
        ig_row = jnp.sum(jnp.where(sub == h, gates_t, 0.0), axis=0, keepdims=True)
        bc_row = jnp.sum(jnp.where(sub == heads + h, cum_t, 0.0), axis=0, keepdims=True)

        m_prev = m_ref[...]
        log_d = jnp.where(causal, bc - bc_row + ig_row, -jnp.inf)
        m_inter = bc + m_prev
        m_t = jnp.maximum(jnp.max(log_d, axis=1, keepdims=True), m_inter)
        dmat = jnp.exp(log_d - m_t)
        inter = jnp.exp(m_inter - m_t)

        q = q_ref[sl, :]
        k = k_ref[sl, :]
        v = v_ref[sl, :]
        scores = _dot_nt(q, k) * (dmat * k_scale)
        c_mat = c_ref[...]
        n_vec = n_ref[...]
        num = _dot(scores.astype(BF16), v) + inter * _dot(q, c_mat.astype(BF16))
        qn = jnp.sum(q.astype(F32) * n_vec, axis=1, keepdims=True)
        den = jnp.sum(scores, axis=1, keepdims=True) + inter * qn
        hidden = num / jnp.maximum(jnp.abs(den), jnp.exp(-m_t))

        b_last = bc[rows - 1:rows, :]
        log_w = b_last - bc + ig
        m_new = jnp.maximum(b_last + m_prev, jnp.max(log_w, axis=0, keepdims=True))
        wts = jnp.exp(log_w - m_new) * k_scale
        decay = jnp.exp(b_last + m_prev - m_new)
        kw = k.astype(F32) * wts
        c_ref[...] = decay * c_mat + _dot_tn(kw.astype(BF16), v)
        n_ref[...] = decay * n_vec + jnp.sum(kw, axis=0, keepdims=True)
        m_ref[...] = m_new

        hn = _head_norm(hidden, gn)
        z = z_ref[sl, :].astype(F32)
        hs = (hn + skip * xc_ref[sl, :].astype(F32)) * (z * _sigmoid(z))
        o_ref[sl, :] = hs.astype(o_ref.dtype)


def _ml_chunk(q, k, v, gates, xc, z, gn_g, skip, *, batch, seq, heads):
    t, inner = q.shape
    dh = inner // heads
    step = _tile(seq, STEP_ROWS)
    rows = _tile(step, CHUNK_ROWS)
    nsteps = seq // step
    blk = pl.BlockSpec((step, dh), lambda b, h, n: (b * nsteps + n, h))
    vec = pl.BlockSpec((1, dh), lambda b, h, n: (0, h))
    kern = functools.partial(_ml_chunk_kernel, rows=rows, n_sub=step // rows, heads=heads,
                             k_scale=float(dh) ** -0.5)
    return pl.pallas_call(
        kern,
        grid=(batch, heads, nsteps),
        in_specs=[blk, blk, blk,
                  pl.BlockSpec((step, gates.shape[1]), lambda b, h, n: (b * nsteps + n, 0)),
                  blk, blk, vec, vec],
        out_specs=blk,
        out_shape=jax.ShapeDtypeStruct((t, inner), BF16),
        scratch_shapes=[pltpu.VMEM((dh, dh), F32),
                        pltpu.VMEM((1, dh), F32),
                        pltpu.VMEM((1, 1), F32)],
        compiler_params=_params("parallel", "parallel", "arbitrary"),
        name="ml_chunk",
    )(q, k, v, gates, xc, z, gn_g.reshape(1, inner), skip.reshape(1, inner))


def kernel(x, positions, norm_mix_g, norm_ffn_g, ret_w_in, ret_gn_g, ret_w_out, ml_w_in, ml_conv_w, ml_conv_b, ml_w_q, ml_w_k, ml_w_v, ml_w_gate, ml_b_gate, ml_gn_g, ml_skip, ml_w_out, ffn_w_gate, ffn_w_up, ffn_w_down, final_g):
    batch, seq, d_model = x.shape
    t = batch * seq
    depth = norm_mix_g.shape[0]
    n_mixers = 2

    qk_dim = d_model
    v_dim = ret_w_out.shape[1]
    dk = qk_dim // RET_HEADS
    dv = v_dim // RET_HEADS

    inv_freq = ROPE_BASE ** (-jnp.arange(dk // 2, dtype=F32) * (2.0 / dk))
    cos, sin = _rope_table(positions.reshape(t), inv_freq)
    log_gamma = jnp.log(1.0 - 2.0 ** (-5.0 - jnp.arange(RET_HEADS, dtype=F32)))

    h = x.reshape(t, d_model)
    u = _rmsnorm(h, norm_mix_g[0])
    out = None
    for i in range(depth):
        j = i // n_mixers
        if i % n_mixers == 0:
            qkvg = _ret_in(u, ret_w_in[j].astype(BF16), cos, sin, dk=dk, qk_dim=qk_dim)
            y = _ret_chunk(qkvg, ret_gn_g[j], log_gamma, batch=batch, seq=seq,
                           dk=dk, dv=dv, qk_dim=qk_dim, v_dim=v_dim)
            w_out = ret_w_out[j]
        else:
            q, k, v, xc, z, gates = _ml_in(u, ml_w_in[j].astype(BF16), ml_conv_w[j], ml_conv_b[j],
                                           ml_w_q[j], ml_w_k[j], ml_w_v[j], ml_w_gate[j],
                                           ml_b_gate[j], seq=seq)
            y = _ml_chunk(q, k, v, gates, xc, z, ml_gn_g[j], ml_skip[j],
                          batch=batch, seq=seq, heads=ML_HEADS)
            w_out = ml_w_out[j]
        h, u = _proj_residual(y, w_out.astype(BF16), h, norm_ffn_g[i])
        mid = _ffn_up(u, ffn_w_gate[i].astype(BF16), ffn_w_up[i].astype(BF16))
        if i + 1 < depth:
            h, u = _proj_residual(mid, ffn_w_down[i].astype(BF16), h, norm_mix_g[i + 1])
        else:
            out = _proj_residual(mid, ffn_w_down[i].astype(BF16), h, final_g, final=True)
    return out.reshape(batch, seq, d_model)
```

```python
import functools

import jax
import jax.numpy as jnp
from jax import lax
from jax.experimental import pallas as pl
from jax.experimental.pallas import tpu as pltpu

F32 = jnp.float32
BF16 = jnp.bfloat16

EPS = 1e-6
ROPE_BASE = 10000.0
RET_HEADS = 8
ML_HEADS = 4
CONV_WIDTH = 4

V7X_VMEM_BYTES = 64 * 1024 * 1024
V7X_LANES = 128
V7X_SUBLANES = 8
V7X_MXU_DIM = 256

CHUNK_ROWS = V7X_MXU_DIM
RET_STEP_ROWS = 1024
ML_STEP_ROWS = 512
RESIDENT_ROWS = 256
VMEM_LIMIT = V7X_VMEM_BYTES - 8 * 1024 * 1024


def _tile(n, pref):
    t = min(pref, n)
    while n % t:
        t //= 2
    return t


def _params(*semantics):
    return pltpu.CompilerParams(dimension_semantics=semantics, vmem_limit_bytes=VMEM_LIMIT)


def _sigmoid(x):
    return 1.0 / (1.0 + jnp.exp(-x))


def _dot(a, b):
    return jnp.dot(a, b, preferred_element_type=F32)


def _dot_nt(a, b):
    return lax.dot_general(a, b, (((1,), (1,)), ((), ())), preferred_element_type=F32)


def _dot_tn(a, b):
    return lax.dot_general(a, b, (((0,), (0,)), ((), ())), preferred_element_type=F32)


def _rms(x, g):
    ms = jnp.mean(x * x, axis=-1, keepdims=True)
    return x * lax.rsqrt(ms + EPS) * g


def _head_norm(o, g):
    mu = jnp.mean(o, axis=-1, keepdims=True)
    d = o - mu
    var = jnp.mean(d * d, axis=-1, keepdims=True)
    return d * lax.rsqrt(var + EPS) * g


def _rmsnorm_kernel(x_ref, g_ref, o_ref):
    o_ref[...] = _rms(x_ref[...], g_ref[...]).astype(o_ref.dtype)


def _rmsnorm(x, g):
    t, d = x.shape
    tm = _tile(t, 512)
    return pl.pallas_call(
        _rmsnorm_kernel,
        grid=(t // tm,),
        in_specs=[pl.BlockSpec((tm, d), lambda i: (i, 0)),
                  pl.BlockSpec((1, d), lambda i: (0, 0))],
        out_specs=pl.BlockSpec((tm, d), lambda i: (i, 0)),
        out_shape=jax.ShapeDtypeStruct((t, d), BF16),
        compiler_params=_params("parallel"),
        name="rmsnorm",
    )(x, g.reshape(1, d))


def _rope_kernel(pos_ref, invf_ref, cos_ref, sin_ref):
    ang = pos_ref[...].astype(F32) * invf_ref[...]
    cos_ref[...] = jnp.cos(ang)
    sin_ref[...] = jnp.sin(ang)


def _rope_table(pos, inv_freq):
    t = pos.shape[0]
    half = inv_freq.shape[0]
    tm = _tile(t, 2048)
    return pl.pallas_call(
        _rope_kernel,
        grid=(t // tm,),
        in_specs=[pl.BlockSpec((tm, 1), lambda i: (i, 0)),
                  pl.BlockSpec((1, half), lambda i: (0, 0))],
        out_specs=[pl.BlockSpec((tm, half), lambda i: (i, 0))] * 2,
        out_shape=[jax.ShapeDtypeStruct((t, half), F32)] * 2,
        compiler_params=_params("parallel"),
        name="rope_table",
    )(pos.reshape(t, 1), inv_freq.reshape(1, half))


def _matmul_kernel(u_ref, w_ref, o_ref):
    o_ref[...] = _dot(u_ref[...], w_ref[...]).astype(o_ref.dtype)


def _matmul(u, w, *, col_start, col_count, name):
    t, d = u.shape
    tm = _tile(t, 1024)
    tn = _tile(col_count, 1024)
    assert col_start % tn == 0
    j0 = col_start // tn
    return pl.pallas_call(
        _matmul_kernel,
        grid=(t // tm, col_count // tn),
        in_specs=[pl.BlockSpec((tm, d), lambda i, j: (i, 0)),
                  pl.BlockSpec((d, tn), lambda i, j: (0, j0 + j))],
        out_specs=pl.BlockSpec((tm, tn), lambda i, j: (i, j)),
        out_shape=jax.ShapeDtypeStruct((t, col_count), BF16),
        compiler_params=_params("parallel", "arbitrary"),
        name=name,
    )(u, w)


def _ret_qk_kernel(u_ref, w_ref, cos_ref, sin_ref, o_ref, *, tn, dk, n_q_blocks, k_scale):
    acc = _dot(u_ref[...], w_ref[...])
    scale = jnp.where(pl.program_id(1) >= n_q_blocks, k_scale, 1.0).astype(F32)
    c = cos_ref[...] * scale
    s = sin_ref[...] * scale
    half = dk // 2
    for hh in range(tn // dk):
        t1 = acc[:, hh * dk:hh * dk + half]
        t2 = acc[:, hh * dk + half:(hh + 1) * dk]
        o_ref[:, hh * dk:hh * dk + half] = (t1 * c - t2 * s).astype(o_ref.dtype)
        o_ref[:, hh * dk + half:(hh + 1) * dk] = (t1 * s + t2 * c).astype(o_ref.dtype)


def _ret_qk(u, w, cos, sin, *, dk, qk_dim):
    t, d = u.shape
    tm = _tile(t, 1024)
    tn = _tile(qk_dim, 1024)
    kern = functools.partial(_ret_qk_kernel, tn=tn, dk=dk, n_q_blocks=qk_dim // tn,
                             k_scale=float(dk) ** -0.5)
    return pl.pallas_call(
        kern,
        grid=(t // tm, 2 * qk_dim // tn),
        in_specs=[pl.BlockSpec((tm, d), lambda i, j: (i, 0)),
                  pl.BlockSpec((d, tn), lambda i, j: (0, j)),
                  pl.BlockSpec((tm, dk // 2), lambda i, j: (i, 0)),
                  pl.BlockSpec((tm, dk // 2), lambda i, j: (i, 0))],
        out_specs=pl.BlockSpec((tm, tn), lambda i, j: (i, j)),
        out_shape=jax.ShapeDtypeStruct((t, 2 * qk_dim), BF16),
        compiler_params=_params("parallel", "arbitrary"),
        name="ret_qk",
    )(u, w, cos, sin)


def _ret_chunk_kernel(lg_ref, q_ref, k_ref, v_ref, o_ref, state_ref, dmat_ref, xi_ref, zeta_ref,
                      *, rows, n_sub):
    lg = lg_ref[pl.program_id(1)]

    @pl.when(pl.program_id(2) == 0)
    def _():
        state_ref[...] = jnp.zeros_like(state_ref)
        ri = lax.broadcasted_iota(jnp.int32, (rows, rows), 0)
        ci = lax.broadcasted_iota(jnp.int32, (rows, rows), 1)
        rel = (ri - ci).astype(F32)
        dmat_ref[...] = jnp.where(rel >= 0.0, jnp.exp(jnp.maximum(rel, 0.0) * lg), 0.0)
        idx = lax.broadcasted_iota(jnp.int32, (rows, 1), 0).astype(F32)
        xi_ref[...] = jnp.exp((idx + 1.0) * lg)
        zeta_ref[...] = jnp.exp((rows - 1.0 - idx) * lg)

    chunk_decay = jnp.exp(jnp.full((1, 1), float(rows), F32) * lg)
    for c in range(n_sub):
        sl = slice(c * rows, (c + 1) * rows)
        q = q_ref[sl, :]
        k = k_ref[sl, :]
        v = v_ref[sl, :]
        state = state_ref[...]
        scores = _dot_nt(q, k) * dmat_ref[...]
        o = _dot(scores.astype(BF16), v) + _dot(q, state.astype(BF16)) * xi_ref[...]
        kz = (k.astype(F32) * zeta_ref[...]).astype(BF16)
        state_ref[...] = chunk_decay * state + _dot_tn(kz, v)
        o_ref[sl, :] = o.astype(o_ref.dtype)


def _ret_chunk(qk, vg, log_gamma, *, batch, seq, dk, dv, qk_dim, v_dim):
    t = qk.shape[0]
    heads = qk_dim // dk
    step = _tile(seq, RET_STEP_ROWS)
    rows = _tile(step, CHUNK_ROWS)
    nsteps = seq // step
    kern = functools.partial(_ret_chunk_kernel, rows=rows, n_sub=step // rows)
    return pl.pallas_call(
        kern,
        grid=(batch, heads, nsteps),
        in_specs=[pl.BlockSpec(memory_space=pltpu.SMEM),
                  pl.BlockSpec((step, dk), lambda b, h, n: (b * nsteps + n, h)),
                  pl.BlockSpec((step, dk), lambda b, h, n: (b * nsteps + n, heads + h)),
                  pl.BlockSpec((step, dv), lambda b, h, n: (b * nsteps + n, h))],
        out_specs=pl.BlockSpec((step, dv), lambda b, h, n: (b * nsteps + n, h)),
        out_shape=jax.ShapeDtypeStruct((t, v_dim), BF16),
        scratch_shapes=[pltpu.VMEM((dk, dv), F32),
                        pltpu.VMEM((rows, rows), F32),
                        pltpu.VMEM((rows, 1), F32),
                        pltpu.VMEM((rows, 1), F32)],
        compiler_params=_params("parallel", "parallel", "arbitrary"),
        name="ret_chunk",
    )(log_gamma, qk, qk, vg)


def _residual_and_norm(y, w_ref, h_ref, ng_ref, out_refs, final):
    hn = h_ref[...] + _dot(y, w_ref[...])
    if final:
        out_refs[0][...] = _rms(hn, ng_ref[...])
    else:
        out_refs[0][...] = hn
        out_refs[1][...] = _rms(hn, ng_ref[...]).astype(out_refs[1].dtype)


def _ffn_down_kernel(y_ref, w_ref, h_ref, ng_ref, *out_refs, final):
    _residual_and_norm(y_ref[...], w_ref, h_ref, ng_ref, out_refs, final)


def _ret_out_kernel(o_ref, g_ref, gn_ref, w_ref, h_ref, ng_ref, hout_ref, u_ref, y_ref, *, heads):
    dv = o_ref.shape[1] // heads
    for hh in range(heads):
        sl = slice(hh * dv, (hh + 1) * dv)
        g = g_ref[:, sl].astype(F32)
        y = _head_norm(o_ref[:, sl].astype(F32), gn_ref[:, sl]) * (g * _sigmoid(g))
        y_ref[:, sl] = y.astype(y_ref.dtype)
    _residual_and_norm(y_ref[...], w_ref, h_ref, ng_ref, (hout_ref, u_ref), False)


def _ml_out_kernel(hid_ref, xc_ref, z_ref, gn_ref, skip_ref, w_ref, h_ref, ng_ref, hout_ref, u_ref,
                   y_ref, *, heads):
    dh = hid_ref.shape[1] // heads
    for hh in range(heads):
        sl = slice(hh * dh, (hh + 1) * dh)
        z = z_ref[:, sl].astype(F32)
        hn = _head_norm(hid_ref[:, sl].astype(F32), gn_ref[:, sl])
        y = (hn + skip_ref[:, sl] * xc_ref[:, sl].astype(F32)) * (z * _sigmoid(z))
        y_ref[:, sl] = y.astype(y_ref.dtype)
    _residual_and_norm(y_ref[...], w_ref, h_ref, ng_ref, (hout_ref, u_ref), False)


def _sublayer_out(kern, acts, vecs, w, h, ng, *, final=False, name):
    t, d = h.shape
    k = w.shape[0]
    tm = _tile(t, RESIDENT_ROWS)
    row_spec = pl.BlockSpec((tm, d), lambda i: (i, 0))
    vec_spec = pl.BlockSpec((1, k), lambda i: (0, 0))
    if final:
        out_specs, out_shape = row_spec, jax.ShapeDtypeStruct((t, d), F32)
    else:
        out_specs = [row_spec, row_spec]
        out_shape = [jax.ShapeDtypeStruct((t, d), F32), jax.ShapeDtypeStruct((t, d), BF16)]
    act_specs = [pl.BlockSpec((tm, k), functools.partial(lambda i, cb: (i, cb), cb=cb))
                 for _, cb in acts]
    scratch = [pltpu.VMEM((tm, k), BF16)] if len(acts) > 1 else []
    return pl.pallas_call(
        kern,
        grid=(t // tm,),
        in_specs=act_specs + [vec_spec] * len(vecs) + [
            pl.BlockSpec((k, d), lambda i: (0, 0), pipeline_mode=pl.Buffered(1)),
            row_spec,
            pl.BlockSpec((1, d), lambda i: (0, 0))],
        out_specs=out_specs,
        out_shape=out_shape,
        scratch_shapes=scratch,
        compiler_params=_params("parallel"),
        name=name,
    )(*[a for a, _ in acts], *[v.reshape(1, k) for v in vecs], w, h, ng.reshape(1, d))


def _ffn_up_kernel(u_ref, wg_ref, wu_ref, o_ref):
    u = u_ref[...]
    a = _dot(u, wg_ref[...])
    b = _dot(u, wu_ref[...])
    o_ref[...] = (a * _sigmoid(a) * b).astype(o_ref.dtype)


def _ffn_up(u, wg, wu):
    t, d = u.shape
    f = wg.shape[1]
    tm = _tile(t, 1024)
    tf = _tile(f, 512)
    return pl.pallas_call(
        _ffn_up_kernel,
        grid=(t // tm, f // tf),
        in_specs=[pl.BlockSpec((tm, d), lambda i, j: (i, 0)),
                  pl.BlockSpec((d, tf), lambda i, j: (0, j)),
                  pl.BlockSpec((d, tf), lambda i, j: (0, j))],
        out_specs=pl.BlockSpec((tm, tf), lambda i, j: (i, j)),
        out_shape=jax.ShapeDtypeStruct((t, f), BF16),
        compiler_params=_params("parallel", "arbitrary"),
        name="ffn_up",
    )(u, wg, wu)


def _ml_in_kernel(u_ref, wx_ref, wz_ref, cw_ref, cb_ref, bq_ref, bk_ref, bv_ref,
                  gq_ref, gk_ref, gv_ref, bg_ref,
                  q_ref, k_ref, v_ref, xc_ref, z_ref, gates_ref,
                  ext_ref, halo_ref, *, tm, tn, seq):
    i = pl.program_id(0)
    j = pl.program_id(1)
    u = u_ref[...]
    xm = _dot(u, wx_ref[...])
    z_ref[...] = _dot(u, wz_ref[...]).astype(z_ref.dtype)

    seq_start = (i * tm) % seq == 0
    ext_ref[0:V7X_SUBLANES, :] = jnp.where(seq_start, 0.0, halo_ref[j])
    ext_ref[V7X_SUBLANES:, :] = xm
    halo_ref[j] = xm[tm - V7X_SUBLANES:, :]
    conv = cb_ref[...]
    for tap in range(CONV_WIDTH):
        off = V7X_SUBLANES - (CONV_WIDTH - 1) + tap
        conv = conv + ext_ref[off:off + tm, :] * cw_ref[tap:tap + 1, :]
    xc = conv * _sigmoid(conv)
    xc_ref[...] = xc.astype(xc_ref.dtype)

    @pl.when(j == 0)
    def _():
        gates_ref[...] = jnp.broadcast_to(bg_ref[...], gates_ref.shape)

    gacc = jnp.zeros(gates_ref.shape, F32)
    for c in range(tn // V7X_MXU_DIM):
        sl = slice(c * V7X_MXU_DIM, (c + 1) * V7X_MXU_DIM)
        xcb = xc[:, sl].astype(BF16)
        xmb = xm[:, sl].astype(BF16)
        q = _dot(xcb, bq_ref[c]).astype(BF16)
        k = _dot(xcb, bk_ref[c]).astype(BF16)
        v = _dot(xmb, bv_ref[c]).astype(BF16)
        q_ref[:, sl] = q
        k_ref[:, sl] = k
        v_ref[:, sl] = v
        gacc = gacc + _dot(q, gq_ref[sl, :]) + _dot(k, gk_ref[sl, :]) + _dot(v, gv_ref[sl, :])
    gates_ref[...] += gacc


def _block_diag_tiles(w, tile):
    nb, bo, bi = w.shape
    per = tile // bi
    wt = jnp.swapaxes(w, 1, 2).reshape(nb // per, per, bi, bo)
    eye = jnp.eye(per, dtype=w.dtype)
    full = jnp.einsum("tbio,bc->tbico", wt, eye)
    return full.reshape(nb // per, tile, tile)


def _ml_in(u, w_in, conv_w, conv_b, w_q, w_k, w_v, w_gate, b_gate, *, seq):
    t, d = u.shape
    inner = w_in.shape[1] // 2
    tm = _tile(seq, 1024)
    tn = 512
    nj = inner // tn
    per_tile = tn // V7X_MXU_DIM
    ngates = w_gate.shape[1]

    def pad_gate(wg):
        return jnp.pad(wg, ((0, 0), (0, V7X_LANES - ngates))).astype(BF16)

    gq, gk, gv = (pad_gate(w_gate[s * inner:(s + 1) * inner]) for s in range(3))
    bg = jnp.pad(b_gate, (0, V7X_LANES - ngates)).reshape(1, V7X_LANES).astype(F32)
    bq, bk, bv = (_block_diag_tiles(w, V7X_MXU_DIM).astype(BF16) for w in (w_q, w_k, w_v))

    col = pl.BlockSpec((tm, tn), lambda i, j: (i, j))
    bd = pl.BlockSpec((per_tile, V7X_MXU_DIM, V7X_MXU_DIM), lambda i, j: (j, 0, 0))
    gw = pl.BlockSpec((tn, V7X_LANES), lambda i, j: (j, 0))
    act = jax.ShapeDtypeStruct((t, inner), BF16)
    kern = functools.partial(_ml_in_kernel, tm=tm, tn=tn, seq=seq)
    return pl.pallas_call(
        kern,
        grid=(t // tm, nj),
        in_specs=[pl.BlockSpec((tm, d), lambda i, j: (i, 0)),
                  pl.BlockSpec((d, tn), lambda i, j: (0, j)),
                  pl.BlockSpec((d, tn), lambda i, j: (0, nj + j)),
                  pl.BlockSpec((CONV_WIDTH, tn), lambda i, j: (0, j)),
                  pl.BlockSpec((1, tn), lambda i, j: (0, j)),
                  bd, bd, bd, gw, gw, gw,
                  pl.BlockSpec((1, V7X_LANES), lambda i, j: (0, 0))],
        out_specs=[col, col, col, col, col,
                   pl.BlockSpec((tm, V7X_LANES), lambda i, j: (i, 0))],
        out_shape=[act, act, act, act, act, jax.ShapeDtypeStruct((t, V7X_LANES), F32)],
        scratch_shapes=[pltpu.VMEM((tm + V7X_SUBLANES, tn), F32),
                        pltpu.VMEM((nj, V7X_SUBLANES, tn), F32)],
        compiler_params=_params("arbitrary", "arbitrary"),
        name="ml_in",
    )(u, w_in, w_in, conv_w, conv_b.reshape(1, inner), bq, bk, bv, gq, gk, gv, bg)


def _log_sigmoid(x):
    return jnp.minimum(x, 0.0) - jnp.log1p(jnp.exp(-jnp.abs(x)))


def _ml_chunk_kernel(q_ref, k_ref, v_ref, gates_ref, o_ref, c_ref, n_ref, m_ref,
                     *, rows, n_sub, heads, k_scale):
    h = pl.program_id(1)

    @pl.when(pl.program_id(2) == 0)
    def _():
        c_ref[...] = jnp.zeros_like(c_ref)
        n_ref[...] = jnp.zeros_like(n_ref)
        m_ref[...] = jnp.zeros_like(m_ref)

    lanes = gates_ref.shape[1]
    ri = lax.broadcasted_iota(jnp.int32, (rows, rows), 0)
    ci = lax.broadcasted_iota(jnp.int32, (rows, rows), 1)
    causal = ri >= ci
    tri = causal.astype(F32)
    lane = lax.broadcasted_iota(jnp.int32, (1, lanes), 1)
    sub = lax.broadcasted_iota(jnp.int32, (lanes, 1), 0)

    for c in range(n_sub):
        sl = slice(c * rows, (c + 1) * rows)
        gates = gates_ref[sl, :]
        log_f = _log_sigmoid(gates)
        cum = jnp.dot(tri, log_f, precision=lax.Precision.HIGHEST,
                      preferred_element_type=F32)
        gates_t = gates.T
        cum_t = cum.T
        ig = jnp.sum(jnp.where(lane == h, gates, 0.0), axis=1, keepdims=True)
        bc = jnp.sum(jnp.where(lane == heads + h, cum, 0.0), axis=1, keepdims=True)
        ig_row = jnp.sum(jnp.where(sub == h, gates_t, 0.0), axis=0, keepdims=True)
        bc_row = jnp.sum(jnp.where(sub == heads + h, cum_t, 0.0), axis=0, keepdims=True)

        m_prev = m_ref[...]
        log_d = jnp.where(causal, bc - bc_row + ig_row, -jnp.inf)
        m_inter = bc + m_prev
        m_t = jnp.maximum(jnp.max(log_d, axis=1, keepdims=True), m_inter)
        dmat = jnp.exp(log_d - m_t)
        inter = jnp.exp(m_inter - m_t)

        q = q_ref[sl, :]
        k = k_ref[sl, :]
        v = v_ref[sl, :]
        scores = _dot_nt(q, k) * (dmat * k_scale)
        c_mat = c_ref[...]
        n_vec = n_ref[...]
        num = _dot(scores.astype(BF16), v) + inter * _dot(q, c_mat.astype(BF16))
        qn = jnp.sum(q.astype(F32) * n_vec, axis=1, keepdims=True)
        den = jnp.sum(scores, axis=1, keepdims=True) + inter * qn
        o_ref[sl, :] = (num / jnp.maximum(jnp.abs(den), jnp.exp(-m_t))).astype(o_ref.dtype)

        b_last = bc[rows - 1:rows, :]
        log_w = b_last - bc + ig
        m_new = jnp.maximum(b_last + m_prev, jnp.max(log_w, axis=0, keepdims=True))
        wts = jnp.exp(log_w - m_new) * k_scale
        decay = jnp.exp(b_last + m_prev - m_new)
        kw = k.astype(F32) * wts
        c_ref[...] = decay * c_mat + _dot_tn(kw.astype(BF16), v)
        n_ref[...] = decay * n_vec + jnp.sum(kw, axis=0, keepdims=True)
        m_ref[...] = m_new


def _ml_chunk(q, k, v, gates, *, batch, seq, heads):
    t, inner = q.shape
    dh = inner // heads
    step = _tile(seq, ML_STEP_ROWS)
    rows = _tile(step, CHUNK_ROWS)
    nsteps = seq // step
    blk = pl.BlockSpec((step, dh), lambda b, h, n: (b * nsteps + n, h))
    kern = functools.partial(_ml_chunk_kernel, rows=rows, n_sub=step // rows, heads=heads,
                             k_scale=float(dh) ** -0.5)
    return pl.pallas_call(
        kern,
        grid=(batch, heads, nsteps),
        in_specs=[blk, blk, blk,
                  pl.BlockSpec((step, gates.shape[1]), lambda b, h, n: (b * nsteps + n, 0))],
        out_specs=blk,
        out_shape=jax.ShapeDtypeStruct((t, inner), BF16),
        scratch_shapes=[pltpu.VMEM((dh, dh), F32),
                        pltpu.VMEM((1, dh), F32),
                        pltpu.VMEM((1, 1), F32)],
        compiler_params=_params("parallel", "parallel", "arbitrary"),
        name="ml_chunk",
    )(q, k, v, gates)


def kernel(x, positions, norm_mix_g, norm_ffn_g, ret_w_in, ret_gn_g, ret_w_out, ml_w_in, ml_conv_w, ml_conv_b, ml_w_q, ml_w_k, ml_w_v, ml_w_gate, ml_b_gate, ml_gn_g, ml_skip, ml_w_out, ffn_w_gate, ffn_w_up, ffn_w_down, final_g):
    batch, seq, d_model = x.shape
    t = batch * seq
    depth = norm_mix_g.shape[0]
    n_mixers = 2

    qk_dim = d_model
    v_dim = ret_w_out.shape[1]
    dk = qk_dim // RET_HEADS
    dv = v_dim // RET_HEADS

    inv_freq = ROPE_BASE ** (-jnp.arange(dk // 2, dtype=F32) * (2.0 / dk))
    cos, sin = _rope_table(positions.reshape(t), inv_freq)
    log_gamma = jnp.log(1.0 - 2.0 ** (-5.0 - jnp.arange(RET_HEADS, dtype=F32)))

    h = x.reshape(t, d_model)
    u = _rmsnorm(h, norm_mix_g[0])
    out = None
    for i in range(depth):
        j = i // n_mixers
        if i % n_mixers == 0:
            w_in = ret_w_in[j].astype(BF16)
            qk = _ret_qk(u, w_in, cos, sin, dk=dk, qk_dim=qk_dim)
            vg = _matmul(u, w_in, col_start=2 * qk_dim, col_count=2 * v_dim, name="ret_vg")
            o = _ret_chunk(qk, vg, log_gamma, batch=batch, seq=seq,
                           dk=dk, dv=dv, qk_dim=qk_dim, v_dim=v_dim)
            h, u = _sublayer_out(functools.partial(_ret_out_kernel, heads=RET_HEADS),
                                 [(o, 0), (vg, 1)], [ret_gn_g[j]],
                                 ret_w_out[j].astype(BF16), h, norm_ffn_g[i], name="ret_out")
        else:
            q, k, v, xc, z, gates = _ml_in(u, ml_w_in[j].astype(BF16), ml_conv_w[j], ml_conv_b[j],
                                           ml_w_q[j], ml_w_k[j], ml_w_v[j], ml_w_gate[j],
                                           ml_b_gate[j], seq=seq)
            hid = _ml_chunk(q, k, v, gates, batch=batch, seq=seq, heads=ML_HEADS)
            h, u = _sublayer_out(functools.partial(_ml_out_kernel, heads=ML_HEADS),
                                 [(hid, 0), (xc, 0), (z, 0)], [ml_gn_g[j], ml_skip[j]],
                                 ml_w_out[j].astype(BF16), h, norm_ffn_g[i], name="ml_out")
        mid = _ffn_up(u, ffn_w_gate[i].astype(BF16), ffn_w_up[i].astype(BF16))
        w_down = ffn_w_down[i].astype(BF16)
        if i + 1 < depth:
            h, u = _sublayer_out(functools.partial(_ffn_down_kernel, final=False), [(mid, 0)], [],
                                 w_down, h, norm_mix_g[i + 1], name="ffn_down")
        else:
            out = _sublayer_out(functools.partial(_ffn_down_kernel, final=True), [(mid, 0)], [],
                                w_down, h, final_g, final=True, name="ffn_down_final")
    return out.reshape(batch, seq, d_model)
```

```python
import functools

import jax
import jax.numpy as jnp
from jax import lax
from jax.experimental import pallas as pl
from jax.experimental.pallas import tpu as pltpu

F32 = jnp.float32
BF16 = jnp.bfloat16

EPS = 1e-6
ROPE_BASE = 10000.0
RET_HEADS = 8
ML_HEADS = 4
CONV_WIDTH = 4

V7X_VMEM_BYTES = 64 * 1024 * 1024
V7X_LANES = 128
V7X_SUBLANES = 8
V7X_MXU_DIM = 256

CHUNK_ROWS = V7X_MXU_DIM
RET_STEP_ROWS = 1024
ML_STEP_ROWS = 512
RESIDENT_ROWS = 256
RET_OUT_GROUPS = 8
ML_OUT_GROUPS = 4
ML_IN_ROW_BLOCKS = 4
VMEM_LIMIT = V7X_VMEM_BYTES - 8 * 1024 * 1024


def _tile(n, pref):
    t = min(pref, n)
    while n % t:
        t //= 2
    return t


def _params(*semantics):
    return pltpu.CompilerParams(dimension_semantics=semantics, vmem_limit_bytes=VMEM_LIMIT)


def _sigmoid(x):
    return 1.0 / (1.0 + jnp.exp(-x))


def _dot(a, b):
    return jnp.dot(a, b, preferred_element_type=F32)


def _dot_nt(a, b):
    return lax.dot_general(a, b, (((1,), (1,)), ((), ())), preferred_element_type=F32)


def _dot_tn(a, b):
    return lax.dot_general(a, b, (((0,), (0,)), ((), ())), preferred_element_type=F32)


def _rms(x, g):
    ms = jnp.mean(x * x, axis=-1, keepdims=True)
    return x * lax.rsqrt(ms + EPS) * g


def _head_norm(o, g):
    mu = jnp.mean(o, axis=-1, keepdims=True)
    d = o - mu
    var = jnp.mean(d * d, axis=-1, keepdims=True)
    return d * lax.rsqrt(var + EPS) * g


def _rmsnorm_kernel(x_ref, g_ref, o_ref):
    o_ref[...] = _rms(x_ref[...], g_ref[...]).astype(o_ref.dtype)


def _rmsnorm(x, g):
    t, d = x.shape
    tm = _tile(t, 512)
    return pl.pallas_call(
        _rmsnorm_kernel,
        grid=(t // tm,),
        in_specs=[pl.BlockSpec((tm, d), lambda i: (i, 0)),
                  pl.BlockSpec((1, d), lambda i: (0, 0))],
        out_specs=pl.BlockSpec((tm, d), lambda i: (i, 0)),
        out_shape=jax.ShapeDtypeStruct((t, d), BF16),
        compiler_params=_params("parallel"),
        name="rmsnorm",
    )(x, g.reshape(1, d))


def _rope_kernel(pos_ref, invf_ref, cos_ref, sin_ref):
    ang = pos_ref[...].astype(F32) * invf_ref[...]
    cos_ref[...] = jnp.cos(ang)
    sin_ref[...] = jnp.sin(ang)


def _rope_table(pos, inv_freq):
    t = pos.shape[0]
    half = inv_freq.shape[0]
    tm = _tile(t, 2048)
    return pl.pallas_call(
        _rope_kernel,
        grid=(t // tm,),
        in_specs=[pl.BlockSpec((tm, 1), lambda i: (i, 0)),
                  pl.BlockSpec((1, half), lambda i: (0, 0))],
        out_specs=[pl.BlockSpec((tm, half), lambda i: (i, 0))] * 2,
        out_shape=[jax.ShapeDtypeStruct((t, half), F32)] * 2,
        compiler_params=_params("parallel"),
        name="rope_table",
    )(pos.reshape(t, 1), inv_freq.reshape(1, half))


def _matmul_kernel(u_ref, w_ref, o_ref):
    o_ref[...] = _dot(u_ref[...], w_ref[...].astype(BF16)).astype(o_ref.dtype)


def _matmul(u, w, *, col_start, col_count, name):
    t, d = u.shape
    tm = _tile(t, 1024)
    tn = _tile(col_count, 1024)
    assert col_start % tn == 0
    j0 = col_start // tn
    return pl.pallas_call(
        _matmul_kernel,
        grid=(t // tm, col_count // tn),
        in_specs=[pl.BlockSpec((tm, d), lambda i, j: (i, 0)),
                  pl.BlockSpec((d, tn), lambda i, j: (0, j0 + j))],
        out_specs=pl.BlockSpec((tm, tn), lambda i, j: (i, j)),
        out_shape=jax.ShapeDtypeStruct((t, col_count), BF16),
        compiler_params=_params("parallel", "arbitrary"),
        name=name,
    )(u, w)


def _ret_qk_kernel(u_ref, w_ref, cos_ref, sin_ref, o_ref, *, tn, dk, n_q_blocks, k_scale):
    acc = _dot(u_ref[...], w_ref[...].astype(BF16))
    scale = jnp.where(pl.program_id(1) >= n_q_blocks, k_scale, 1.0).astype(F32)
    c = cos_ref[...] * scale
    s = sin_ref[...] * scale
    half = dk // 2
    for hh in range(tn // dk):
        t1 = acc[:, hh * dk:hh * dk + half]
        t2 = acc[:, hh * dk + half:(hh + 1) * dk]
        o_ref[:, hh * dk:hh * dk + half] = (t1 * c - t2 * s).astype(o_ref.dtype)
        o_ref[:, hh * dk + half:(hh + 1) * dk] = (t1 * s + t2 * c).astype(o_ref.dtype)


def _ret_qk(u, w, cos, sin, *, dk, qk_dim):
    t, d = u.shape
    tm = _tile(t, 1024)
    tn = _tile(qk_dim, 1024)
    kern = functools.partial(_ret_qk_kernel, tn=tn, dk=dk, n_q_blocks=qk_dim // tn,
                             k_scale=float(dk) ** -0.5)
    return pl.pallas_call(
        kern,
        grid=(t // tm, 2 * qk_dim // tn),
        in_specs=[pl.BlockSpec((tm, d), lambda i, j: (i, 0)),
                  pl.BlockSpec((d, tn), lambda i, j: (0, j)),
                  pl.BlockSpec((tm, dk // 2), lambda i, j: (i, 0)),
                  pl.BlockSpec((tm, dk // 2), lambda i, j: (i, 0))],
        out_specs=pl.BlockSpec((tm, tn), lambda i, j: (i, j)),
        out_shape=jax.ShapeDtypeStruct((t, 2 * qk_dim), BF16),
        compiler_params=_params("parallel", "arbitrary"),
        name="ret_qk",
    )(u, w, cos, sin)


def _ret_chunk_kernel(lg_ref, q_ref, k_ref, v_ref, o_ref, state_ref, dmat_ref, xi_ref, zeta_ref,
                      *, rows, n_sub):
    lg = lg_ref[pl.program_id(1)]

    @pl.when(pl.program_id(2) == 0)
    def _():
        state_ref[...] = jnp.zeros_like(state_ref)
        ri = lax.broadcasted_iota(jnp.int32, (rows, rows), 0)
        ci = lax.broadcasted_iota(jnp.int32, (rows, rows), 1)
        rel = (ri - ci).astype(F32)
        dmat_ref[...] = jnp.where(rel >= 0.0, jnp.exp(jnp.maximum(rel, 0.0) * lg), 0.0)
        idx = lax.broadcasted_iota(jnp.int32, (rows, 1), 0).astype(F32)
        xi_ref[...] = jnp.exp((idx + 1.0) * lg)
        zeta_ref[...] = jnp.exp((rows - 1.0 - idx) * lg)

    chunk_decay = jnp.exp(jnp.full((1, 1), float(rows), F32) * lg)
    for c in range(n_sub):
        sl = slice(c * rows, (c + 1) * rows)
        q = q_ref[sl, :]
        k = k_ref[sl, :]
        v = v_ref[sl, :]
        state = state_ref[...]
        scores = _dot_nt(q, k) * dmat_ref[...]
        o = _dot(scores.astype(BF16), v) + _dot(q, state.astype(BF16)) * xi_ref[...]
        kz = (k.astype(F32) * zeta_ref[...]).astype(BF16)
        state_ref[...] = chunk_decay * state + _dot_tn(kz, v)
        o_ref[sl, :] = o.astype(o_ref.dtype)


def _ret_chunk(qk, vg, log_gamma, *, batch, seq, dk, dv, qk_dim, v_dim):
    t = qk.shape[0]
    heads = qk_dim // dk
    step = _tile(seq, RET_STEP_ROWS)
    rows = _tile(step, CHUNK_ROWS)
    nsteps = seq // step
    kern = functools.partial(_ret_chunk_kernel, rows=rows, n_sub=step // rows)
    return pl.pallas_call(
        kern,
        grid=(batch, heads, nsteps),
        in_specs=[pl.BlockSpec(memory_space=pltpu.SMEM),
                  pl.BlockSpec((step, dk), lambda b, h, n: (b * nsteps + n, h)),
                  pl.BlockSpec((step, dk), lambda b, h, n: (b * nsteps + n, heads + h)),
                  pl.BlockSpec((step, dv), lambda b, h, n: (b * nsteps + n, h))],
        out_specs=pl.BlockSpec((step, dv), lambda b, h, n: (b * nsteps + n, h)),
        out_shape=jax.ShapeDtypeStruct((t, v_dim), BF16),
        scratch_shapes=[pltpu.VMEM((dk, dv), F32),
                        pltpu.VMEM((rows, rows), F32),
                        pltpu.VMEM((rows, 1), F32),
                        pltpu.VMEM((rows, 1), F32)],
        compiler_params=_params("parallel", "parallel", "arbitrary"),
        name="ret_chunk",
    )(log_gamma, qk, qk, vg)


def _residual_and_norm(y, w_ref, h_ref, ng_ref, out_refs, final):
    hn = h_ref[...] + _dot(y, w_ref[...])
    if final:
        out_refs[0][...] = _rms(hn, ng_ref[...])
    else:
        out_refs[0][...] = hn
        out_refs[1][...] = _rms(hn, ng_ref[...]).astype(out_refs[1].dtype)


def _ffn_down_kernel(y_ref, w_ref, h_ref, ng_ref, *out_refs, final):
    _residual_and_norm(y_ref[...], w_ref, h_ref, ng_ref, out_refs, final)


def _project_head_groups(build_group, groups, w_ref, h_ref, ng_ref, hout_ref, u_ref):
    gk = w_ref.shape[0] // groups
    acc = h_ref[...]
    for c in range(groups):
        acc = acc + _dot(build_group(c), w_ref[c * gk:(c + 1) * gk, :])
    hout_ref[...] = acc
    u_ref[...] = _rms(acc, ng_ref[...]).astype(u_ref.dtype)


def _ret_out_kernel(o_ref, g_ref, gn_ref, w_ref, h_ref, ng_ref, hout_ref, u_ref, *, heads):
    dv = o_ref.shape[1] // heads
    per = heads // RET_OUT_GROUPS

    def build_group(c):
        ys = []
        for hh in range(c * per, (c + 1) * per):
            sl = slice(hh * dv, (hh + 1) * dv)
            g = g_ref[:, sl].astype(F32)
            y = _head_norm(o_ref[:, sl].astype(F32), gn_ref[:, sl]) * (g * _sigmoid(g))
            ys.append(y.astype(BF16))
        return jnp.concatenate(ys, axis=-1)

    _project_head_groups(build_group, RET_OUT_GROUPS, w_ref, h_ref, ng_ref, hout_ref, u_ref)


def _ml_out_kernel(hid_ref, xc_ref, z_ref, gn_ref, skip_ref, w_ref, h_ref, ng_ref, hout_ref, u_ref,
                   *, heads):
    dh = hid_ref.shape[1] // heads
    per = heads // ML_OUT_GROUPS

    def build_group(c):
        ys = []
        for hh in range(c * per, (c + 1) * per):
            sl = slice(hh * dh, (hh + 1) * dh)
            z = z_ref[:, sl].astype(F32)
            hn = _head_norm(hid_ref[:, sl].astype(F32), gn_ref[:, sl])
            y = (hn + skip_ref[:, sl] * xc_ref[:, sl].astype(F32)) * (z * _sigmoid(z))
            ys.append(y.astype(BF16))
        return jnp.concatenate(ys, axis=-1)

    _project_head_groups(build_group, ML_OUT_GROUPS, w_ref, h_ref, ng_ref, hout_ref, u_ref)


def _sublayer_out(kern, acts, vecs, w, h, ng, *, final=False, name):
    t, d = h.shape
    k = w.shape[0]
    tm = _tile(t, RESIDENT_ROWS)
    row_spec = pl.BlockSpec((tm, d), lambda i: (i, 0))
    vec_spec = pl.BlockSpec((1, k), lambda i: (0, 0))
    if final:
        out_specs, out_shape = row_spec, jax.ShapeDtypeStruct((t, d), F32)
    else:
        out_specs = [row_spec, row_spec]
        out_shape = [jax.ShapeDtypeStruct((t, d), F32), jax.ShapeDtypeStruct((t, d), BF16)]
    act_specs = [pl.BlockSpec((tm, k), functools.partial(lambda i, cb: (i, cb), cb=cb))
                 for _, cb in acts]
    return pl.pallas_call(
        kern,
        grid=(t // tm,),
        in_specs=act_specs + [vec_spec] * len(vecs) + [
            pl.BlockSpec((k, d), lambda i: (0, 0), pipeline_mode=pl.Buffered(1)),
            row_spec,
            pl.BlockSpec((1, d), lambda i: (0, 0))],
        out_specs=out_specs,
        out_shape=out_shape,
        compiler_params=_params("parallel"),
        name=name,
    )(*[a for a, _ in acts], *[v.reshape(1, k) for v in vecs], w, h, ng.reshape(1, d))


def _ffn_up_kernel(u_ref, wg_ref, wu_ref, o_ref):
    u = u_ref[...]
    a = _dot(u, wg_ref[...].astype(BF16))
    b = _dot(u, wu_ref[...].astype(BF16))
    o_ref[...] = (a * _sigmoid(a) * b).astype(o_ref.dtype)


def _ffn_up(u, wg, wu):
    t, d = u.shape
    f = wg.shape[1]
    tm = _tile(t, 1024)
    tf = _tile(f, 512)
    return pl.pallas_call(
        _ffn_up_kernel,
        grid=(t // tm, f // tf),
        in_specs=[pl.BlockSpec((tm, d), lambda i, j: (i, 0)),
                  pl.BlockSpec((d, tf), lambda i, j: (0, j)),
                  pl.BlockSpec((d, tf), lambda i, j: (0, j))],
        out_specs=pl.BlockSpec((tm, tf), lambda i, j: (i, j)),
        out_shape=jax.ShapeDtypeStruct((t, f), BF16),
        compiler_params=_params("parallel", "arbitrary"),
        name="ffn_up",
    )(u, wg, wu)


def _ml_in_kernel(u_ref, wx_ref, wz_ref, cw_ref, cb_ref, bq_ref, bk_ref, bv_ref,
                  gq_ref, gk_ref, gv_ref, bg_ref,
                  q_ref, k_ref, v_ref, xc_ref, z_ref, gates_ref,
                  ext_ref, halo_ref, *, tm, tn, seq):
    i = pl.program_id(0)
    j = pl.program_id(1)
    hs = V7X_SUBLANES

    @pl.when(i == 0)
    def _():
        halo_ref[j] = jnp.zeros(halo_ref.shape[1:], F32)

    seq_start = (i * tm) % seq == 0
    ext_ref[0:hs, :] = jnp.where(seq_start, 0.0, halo_ref[j])
    rb = tm // ML_IN_ROW_BLOCKS
    wx = wx_ref[...].astype(BF16)
    wz = wz_ref[...].astype(BF16)
    for r in range(ML_IN_ROW_BLOCKS):
        rows = slice(r * rb, (r + 1) * rb)
        u = u_ref[rows, :]
        xm = _dot(u, wx)
        ext_ref[hs + r * rb:hs + (r + 1) * rb, :] = xm
        z_ref[rows, :] = _dot(u, wz).astype(z_ref.dtype)
        conv = cb_ref[...]
        for tap in range(CONV_WIDTH):
            off = hs - (CONV_WIDTH - 1) + tap + r * rb
            conv = conv + ext_ref[off:off + rb, :] * cw_ref[tap:tap + 1, :]
        xc = conv * _sigmoid(conv)
        xc_ref[rows, :] = xc.astype(xc_ref.dtype)
        gacc = jnp.where(j == 0, bg_ref[...], gates_ref[rows, :])
        for c in range(tn // V7X_MXU_DIM):
            sl = slice(c * V7X_MXU_DIM, (c + 1) * V7X_MXU_DIM)
            xcb = xc[:, sl].astype(BF16)
            xmb = xm[:, sl].astype(BF16)
            q = _dot(xcb, bq_ref[c]).astype(BF16)
            k = _dot(xcb, bk_ref[c]).astype(BF16)
            v = _dot(xmb, bv_ref[c]).astype(BF16)
            q_ref[rows, sl] = q
            k_ref[rows, sl] = k
            v_ref[rows, sl] = v
            gacc = gacc + _dot(q, gq_ref[sl, :]) + _dot(k, gk_ref[sl, :]) + _dot(v, gv_ref[sl, :])
        gates_ref[rows, :] = gacc
    halo_ref[j] = ext_ref[tm:tm + hs, :]


def _block_diag_tiles(w, tile):
    nb, bo, bi = w.shape
    per = tile // bi
    wt = jnp.swapaxes(w, 1, 2).reshape(nb // per, per, bi, bo)
    eye = jnp.eye(per, dtype=w.dtype)
    full = jnp.einsum("tbio,bc->tbico", wt, eye)
    return full.reshape(nb // per, tile, tile)


def _ml_in(u, w_in, conv_w, conv_b, w_q, w_k, w_v, w_gate, b_gate, *, seq):
    t, d = u.shape
    inner = w_in.shape[1] // 2
    tm = _tile(seq, 1024)
    tn = 512
    nj = inner // tn
    per_tile = tn // V7X_MXU_DIM
    ngates = w_gate.shape[1]

    def pad_gate(wg):
        return jnp.pad(wg, ((0, 0), (0, V7X_LANES - ngates))).astype(BF16)

    gq, gk, gv = (pad_gate(w_gate[s * inner:(s + 1) * inner]) for s in range(3))
    bg = jnp.pad(b_gate, (0, V7X_LANES - ngates)).reshape(1, V7X_LANES).astype(F32)
    bq, bk, bv = (_block_diag_tiles(w, V7X_MXU_DIM).astype(BF16) for w in (w_q, w_k, w_v))

    col = pl.BlockSpec((tm, tn), lambda i, j: (i, j))
    bd = pl.BlockSpec((per_tile, V7X_MXU_DIM, V7X_MXU_DIM), lambda i, j: (j, 0, 0))
    gw = pl.BlockSpec((tn, V7X_LANES), lambda i, j: (j, 0))
    act = jax.ShapeDtypeStruct((t, inner), BF16)
    kern = functools.partial(_ml_in_kernel, tm=tm, tn=tn, seq=seq)
    return pl.pallas_call(
        kern,
        grid=(t // tm, nj),
        in_specs=[pl.BlockSpec((tm, d), lambda i, j: (i, 0)),
                  pl.BlockSpec((d, tn), lambda i, j: (0, j)),
                  pl.BlockSpec((d, tn), lambda i, j: (0, nj + j)),
                  pl.BlockSpec((CONV_WIDTH, tn), lambda i, j: (0, j)),
                  pl.BlockSpec((1, tn), lambda i, j: (0, j)),
                  bd, bd, bd, gw, gw, gw,
                  pl.BlockSpec((1, V7X_LANES), lambda i, j: (0, 0))],
        out_specs=[col, col, col, col, col,
                   pl.BlockSpec((tm, V7X_LANES), lambda i, j: (i, 0))],
        out_shape=[act, act, act, act, act, jax.ShapeDtypeStruct((t, V7X_LANES), F32)],
        scratch_shapes=[pltpu.VMEM((tm + V7X_SUBLANES, tn), F32),
                        pltpu.VMEM((nj, V7X_SUBLANES, tn), F32)],
        compiler_params=_params("arbitrary", "arbitrary"),
        name="ml_in",
    )(u, w_in, w_in, conv_w, conv_b.reshape(1, inner), bq, bk, bv, gq, gk, gv, bg)


def _log_sigmoid(x):
    return jnp.minimum(x, 0.0) - jnp.log1p(jnp.exp(-jnp.abs(x)))


def _gate_prep_kernel(gates_ref, cum_ref, gates_t_ref, cum_t_ref, *, rows, n_sub, keep):
    ri = lax.broadcasted_iota(jnp.int32, (rows, rows), 0)
    ci = lax.broadcasted_iota(jnp.int32, (rows, rows), 1)
    tri = (ri >= ci).astype(BF16)
    for c in range(n_sub):
        sl = slice(c * rows, (c + 1) * rows)
        gates = gates_ref[sl, :]
        log_f = _log_sigmoid(gates)
        hi = log_f.astype(BF16)
        r1 = log_f - hi.astype(F32)
        mid = r1.astype(BF16)
        lo = (r1 - mid.astype(F32)).astype(BF16)
        cum = _dot(tri, hi) + _dot(tri, mid) + _dot(tri, lo)
        cum_ref[sl, :] = cum
        gates_t_ref[:, sl] = gates.T[:keep, :]
        cum_t_ref[:, sl] = cum.T[:keep, :]


def _gate_prep(gates, *, seq):
    t, lanes = gates.shape
    step = _tile(seq, 1024)
    rows = _tile(step, CHUNK_ROWS)
    keep = V7X_SUBLANES
    kern = functools.partial(_gate_prep_kernel, rows=rows, n_sub=step // rows, keep=keep)
    return pl.pallas_call(
        kern,
        grid=(t // step,),
        in_specs=[pl.BlockSpec((step, lanes), lambda i: (i, 0))],
        out_specs=[pl.BlockSpec((step, lanes), lambda i: (i, 0)),
                   pl.BlockSpec((keep, step), lambda i: (0, i)),
                   pl.BlockSpec((keep, step), lambda i: (0, i))],
        out_shape=[jax.ShapeDtypeStruct((t, lanes), F32),
                   jax.ShapeDtypeStruct((keep, t), F32),
                   jax.ShapeDtypeStruct((keep, t), F32)],
        compiler_params=_params("parallel"),
        name="gate_prep",
    )(gates)


def _ml_chunk_kernel(q_ref, k_ref, v_ref, gates_ref, cum_ref, gates_t_ref, cum_t_ref, o_ref,
                     c_ref, cb_ref, n_ref, m_ref, *, rows, n_sub, heads, k_scale):
    h = pl.program_id(1)

    @pl.when(pl.program_id(2) == 0)
    def _():
        c_ref[...] = jnp.zeros_like(c_ref)
        cb_ref[...] = jnp.zeros_like(cb_ref)
        n_ref[...] = jnp.zeros_like(n_ref)
        m_ref[...] = jnp.zeros_like(m_ref)

    lanes = gates_ref.shape[1]
    ri = lax.broadcasted_iota(jnp.int32, (rows, rows), 0)
    ci = lax.broadcasted_iota(jnp.int32, (rows, rows), 1)
    causal = ri >= ci
    lane = lax.broadcasted_iota(jnp.int32, (1, lanes), 1)

    for c in range(n_sub):
        sl = slice(c * rows, (c + 1) * rows)
        ig = jnp.sum(jnp.where(lane == h, gates_ref[sl, :], 0.0), axis=1, keepdims=True)
        bc = jnp.sum(jnp.where(lane == heads + h, cum_ref[sl, :], 0.0), axis=1, keepdims=True)
        ig_row = gates_t_ref[pl.ds(h, 1), sl]
        bc_row = cum_t_ref[pl.ds(heads + h, 1), sl]

        m_prev = m_ref[...]
        log_d = jnp.where(causal, bc - bc_row + ig_row, -jnp.inf)
        m_inter = bc + m_prev
        m_t = jnp.maximum(jnp.max(log_d, axis=1, keepdims=True), m_inter)
        dmat = jnp.exp(log_d - m_t)
        inter = jnp.exp(m_inter - m_t)

        q = q_ref[sl, :]
        k = k_ref[sl, :]
        v = v_ref[sl, :]
        scores = _dot_nt(q, k) * (dmat * k_scale)
        n_vec = n_ref[...]
        num = _dot(scores.astype(BF16), v) + inter * _dot(q, cb_ref[...])
        qn = jnp.sum(q.astype(F32) * n_vec, axis=1, keepdims=True)
        den = jnp.sum(scores, axis=1, keepdims=True) + inter * qn
        o_ref[sl, :] = (num / jnp.maximum(jnp.abs(den), jnp.exp(-m_t))).astype(o_ref.dtype)

        b_last = bc[rows - 1:rows, :]
        log_w = b_last - bc + ig
        m_new = jnp.maximum(b_last + m_prev, jnp.max(log_w, axis=0, keepdims=True))
        wts = jnp.exp(log_w - m_new) * k_scale
        decay = jnp.exp(b_last + m_prev - m_new)
        kw = k.astype(F32) * wts
        c_new = decay * c_ref[...] + _dot_tn(kw.astype(BF16), v)
        c_ref[...] = c_new
        cb_ref[...] = c_new.astype(BF16)
        n_ref[...] = decay * n_vec + jnp.sum(kw, axis=0, keepdims=True)
        m_ref[...] = m_new


def _ml_chunk(q, k, v, gates, cum, gates_t, cum_t, *, batch, seq, heads):
    t, inner = q.shape
    dh = inner // heads
    step = _tile(seq, ML_STEP_ROWS)
    rows = _tile(step, CHUNK_ROWS)
    nsteps = seq // step
    blk = pl.BlockSpec((step, dh), lambda b, h, n: (b * nsteps + n, h))
    col = pl.BlockSpec((step, gates.shape[1]), lambda b, h, n: (b * nsteps + n, 0))
    row = pl.BlockSpec((gates_t.shape[0], step), lambda b, h, n: (0, b * nsteps + n))
    kern = functools.partial(_ml_chunk_kernel, rows=rows, n_sub=step // rows, heads=heads,
                             k_scale=float(dh) ** -0.5)
    return pl.pallas_call(
        kern,
        grid=(batch, heads, nsteps),
        in_specs=[blk, blk, blk, col, col, row, row],
        out_specs=blk,
        out_shape=jax.ShapeDtypeStruct((t, inner), BF16),
        scratch_shapes=[pltpu.VMEM((dh, dh), F32),
                        pltpu.VMEM((dh, dh), BF16),
                        pltpu.VMEM((1, dh), F32),
                        pltpu.VMEM((1, 1), F32)],
        compiler_params=_params("parallel", "parallel", "arbitrary"),
        name="ml_chunk",
    )(q, k, v, gates, cum, gates_t, cum_t)


def kernel(x, positions, norm_mix_g, norm_ffn_g, ret_w_in, ret_gn_g, ret_w_out, ml_w_in, ml_conv_w, ml_conv_b, ml_w_q, ml_w_k, ml_w_v, ml_w_gate, ml_b_gate, ml_gn_g, ml_skip, ml_w_out, ffn_w_gate, ffn_w_up, ffn_w_down, final_g):
    batch, seq, d_model = x.shape
    t = batch * seq
    depth = norm_mix_g.shape[0]
    n_mixers = 2

    qk_dim = d_model
    v_dim = ret_w_out.shape[1]
    dk = qk_dim // RET_HEADS
    dv = v_dim // RET_HEADS

    inv_freq = ROPE_BASE ** (-jnp.arange(dk // 2, dtype=F32) * (2.0 / dk))
    cos, sin = _rope_table(positions.reshape(t), inv_freq)
    log_gamma = jnp.log(1.0 - 2.0 ** (-5.0 - jnp.arange(RET_HEADS, dtype=F32)))

    h = x.reshape(t, d_model)
    u = _rmsnorm(h, norm_mix_g[0])
    out = None
    for i in range(depth):
        j = i // n_mixers
        if i % n_mixers == 0:
            w_in = ret_w_in[j]
            qk = _ret_qk(u, w_in, cos, sin, dk=dk, qk_dim=qk_dim)
            vg = _matmul(u, w_in, col_start=2 * qk_dim, col_count=2 * v_dim, name="ret_vg")
            o = _ret_chunk(qk, vg, log_gamma, batch=batch, seq=seq,
                           dk=dk, dv=dv, qk_dim=qk_dim, v_dim=v_dim)
            h, u = _sublayer_out(functools.partial(_ret_out_kernel, heads=RET_HEADS),
                                 [(o, 0), (vg, 1)], [ret_gn_g[j]],
                                 ret_w_out[j].astype(BF16), h, norm_ffn_g[i], name="ret_out")
        else:
            q, k, v, xc, z, gates = _ml_in(u, ml_w_in[j], ml_conv_w[j], ml_conv_b[j],
                                           ml_w_q[j], ml_w_k[j], ml_w_v[j], ml_w_gate[j],
                                           ml_b_gate[j], seq=seq)
            cum, gates_t, cum_t = _gate_prep(gates, seq=seq)
            hid = _ml_chunk(q, k, v, gates, cum, gates_t, cum_t, batch=batch, seq=seq, heads=ML_HEADS)
            h, u = _sublayer_out(functools.partial(_ml_out_kernel, heads=ML_HEADS),
                                 [(hid, 0), (xc, 0), (z, 0)], [ml_gn_g[j], ml_skip[j]],
                                 ml_w_out[j].astype(BF16), h, norm_ffn_g[i], name="ml_out")
        mid = _ffn_up(u, ffn_w_gate[i], ffn_w_up[i])
        w_down = ffn_w_down[i].astype(BF16)
        if i + 1 < depth:
            h, u = _sublayer_out(functools.partial(_ffn_down_kernel, final=False), [(mid, 0)], [],
                                 w_down, h, norm_mix_g[i + 1], name="ffn_down")
        else:
            out = _sublayer_out(functools.partial(_ffn_down_kernel, final=True), [(mid, 0)], [],
                                w_down, h, final_g, final=True, name="ffn_down_final")
    return out.reshape(batch, seq, d_model)
```

```python
import functools

import jax
import jax.numpy as jnp
from jax import lax
from jax.experimental import pallas as pl
from jax.experimental.pallas import tpu as pltpu

F32 = jnp.float32
BF16 = jnp.bfloat16

EPS = 1e-6
ROPE_BASE = 10000.0
RET_HEADS = 8
ML_HEADS = 4
CONV_WIDTH = 4

V7X_VMEM_BYTES = 64 * 1024 * 1024
V7X_LANES = 128
V7X_SUBLANES = 8
V7X_MXU_DIM = 256

CHUNK_ROWS = V7X_MXU_DIM
RET_STEP_ROWS = 1024
ML_STEP_ROWS = 1024
STREAM_ROWS = 1024
MATMUL_ROWS = 2048
RESIDENT_ROWS = 256
RET_OUT_GROUPS = 8
ML_OUT_GROUPS = 4
ML_IN_ROW_BLOCKS = 4
VMEM_LIMIT = V7X_VMEM_BYTES - 8 * 1024 * 1024


def _tile(n, pref):
    t = min(pref, n)
    while n % t:
        t //= 2
    return t


def _params(*semantics):
    return pltpu.CompilerParams(dimension_semantics=semantics, vmem_limit_bytes=VMEM_LIMIT)


def _sigmoid(x):
    return 1.0 / (1.0 + jnp.exp(-x))


def _dot(a, b):
    return jnp.dot(a, b, preferred_element_type=F32)


def _dot_nt(a, b):
    return lax.dot_general(a, b, (((1,), (1,)), ((), ())), preferred_element_type=F32)


def _dot_tn(a, b):
    return lax.dot_general(a, b, (((0,), (0,)), ((), ())), preferred_element_type=F32)


def _rms(x, g):
    ms = jnp.mean(x * x, axis=-1, keepdims=True)
    return x * lax.rsqrt(ms + EPS) * g


def _head_norm(o, g):
    mu = jnp.mean(o, axis=-1, keepdims=True)
    d = o - mu
    var = jnp.mean(d * d, axis=-1, keepdims=True)
    return d * lax.rsqrt(var + EPS) * g


def _rmsnorm_kernel(x_ref, g_ref, o_ref):
    o_ref[...] = _rms(x_ref[...], g_ref[...]).astype(o_ref.dtype)


def _rmsnorm(x, g):
    t, d = x.shape
    tm = _tile(t, 512)
    return pl.pallas_call(
        _rmsnorm_kernel,
        grid=(t // tm,),
        in_specs=[pl.BlockSpec((tm, d), lambda i: (i, 0)),
                  pl.BlockSpec((1, d), lambda i: (0, 0))],
        out_specs=pl.BlockSpec((tm, d), lambda i: (i, 0)),
        out_shape=jax.ShapeDtypeStruct((t, d), BF16),
        compiler_params=_params("parallel"),
        name="rmsnorm",
    )(x, g.reshape(1, d))


def _rope_kernel(pos_ref, invf_ref, cos_ref, sin_ref):
    ang = pos_ref[...].astype(F32) * invf_ref[...]
    cos_ref[...] = jnp.cos(ang)
    sin_ref[...] = jnp.sin(ang)


def _rope_table(pos, inv_freq):
    t = pos.shape[0]
    half = inv_freq.shape[0]
    tm = _tile(t, 2048)
    return pl.pallas_call(
        _rope_kernel,
        grid=(t // tm,),
        in_specs=[pl.BlockSpec((tm, 1), lambda i: (i, 0)),
                  pl.BlockSpec((1, half), lambda i: (0, 0))],
        out_specs=[pl.BlockSpec((tm, half), lambda i: (i, 0))] * 2,
        out_shape=[jax.ShapeDtypeStruct((t, half), F32)] * 2,
        compiler_params=_params("parallel"),
        name="rope_table",
    )(pos.reshape(t, 1), inv_freq.reshape(1, half))


def _matmul_kernel(u_ref, w_ref, o_ref):
    o_ref[...] = _dot(u_ref[...], w_ref[...].astype(BF16)).astype(o_ref.dtype)


def _matmul(u, w, layer, *, col_start, col_count, name):
    t, d = u.shape
    tm = _tile(t, MATMUL_ROWS)
    tn = _tile(col_count, 1024)
    assert col_start % tn == 0
    j0 = col_start // tn
    return pl.pallas_call(
        _matmul_kernel,
        grid=(t // tm, col_count // tn),
        in_specs=[pl.BlockSpec((tm, d), lambda i, j: (i, 0)),
                  pl.BlockSpec((None, d, tn), lambda i, j: (layer, 0, j0 + j))],
        out_specs=pl.BlockSpec((tm, tn), lambda i, j: (i, j)),
        out_shape=jax.ShapeDtypeStruct((t, col_count), BF16),
        compiler_params=_params("parallel", "arbitrary"),
        name=name,
    )(u, w)


def _ret_qk_kernel(u_ref, w_ref, cos_ref, sin_ref, o_ref, *, tn, dk, n_q_blocks, k_scale):
    acc = _dot(u_ref[...], w_ref[...].astype(BF16))
    scale = jnp.where(pl.program_id(1) >= n_q_blocks, k_scale, 1.0).astype(F32)
    c = cos_ref[...] * scale
    s = sin_ref[...] * scale
    half = dk // 2
    for hh in range(tn // dk):
        t1 = acc[:, hh * dk:hh * dk + half]
        t2 = acc[:, hh * dk + half:(hh + 1) * dk]
        o_ref[:, hh * dk:hh * dk + half] = (t1 * c - t2 * s).astype(o_ref.dtype)
        o_ref[:, hh * dk + half:(hh + 1) * dk] = (t1 * s + t2 * c).astype(o_ref.dtype)


def _ret_qk(u, w, layer, cos, sin, *, dk, qk_dim):
    t, d = u.shape
    tm = _tile(t, STREAM_ROWS)
    tn = _tile(qk_dim, 1024)
    kern = functools.partial(_ret_qk_kernel, tn=tn, dk=dk, n_q_blocks=qk_dim // tn,
                             k_scale=float(dk) ** -0.5)
    return pl.pallas_call(
        kern,
        grid=(t // tm, 2 * qk_dim // tn),
        in_specs=[pl.BlockSpec((tm, d), lambda i, j: (i, 0)),
                  pl.BlockSpec((None, d, tn), lambda i, j: (layer, 0, j)),
                  pl.BlockSpec((tm, dk // 2), lambda i, j: (i, 0)),
                  pl.BlockSpec((tm, dk // 2), lambda i, j: (i, 0))],
        out_specs=pl.BlockSpec((tm, tn), lambda i, j: (i, j)),
        out_shape=jax.ShapeDtypeStruct((t, 2 * qk_dim), BF16),
        compiler_params=_params("parallel", "arbitrary"),
        name="ret_qk",
    )(u, w, cos, sin)


def _ret_chunk_kernel(lg_ref, q_ref, k_ref, v_ref, o_ref, state_ref, dmat_ref, xi_ref, zeta_ref,
                      *, rows, n_sub):
    lg = lg_ref[pl.program_id(1)]

    @pl.when(pl.program_id(2) == 0)
    def _():
        state_ref[...] = jnp.zeros_like(state_ref)
        ri = lax.broadcasted_iota(jnp.int32, (rows, rows), 0)
        ci = lax.broadcasted_iota(jnp.int32, (rows, rows), 1)
        rel = (ri - ci).astype(F32)
        dmat_ref[...] = jnp.where(rel >= 0.0, jnp.exp(jnp.maximum(rel, 0.0) * lg), 0.0)
        idx = lax.broadcasted_iota(jnp.int32, (rows, 1), 0).astype(F32)
        xi_ref[...] = jnp.exp((idx + 1.0) * lg)
        zeta_ref[...] = jnp.exp((rows - 1.0 - idx) * lg)

    chunk_decay = jnp.exp(jnp.full((1, 1), float(rows), F32) * lg)
    for c in range(n_sub):
        sl = slice(c * rows, (c + 1) * rows)
        q = q_ref[sl, :]
        k = k_ref[sl, :]
        v = v_ref[sl, :]
        state = state_ref[...]
        scores = _dot_nt(q, k) * dmat_ref[...]
        o = _dot(scores.astype(BF16), v) + _dot(q, state.astype(BF16)) * xi_ref[...]
        kz = (k.astype(F32) * zeta_ref[...]).astype(BF16)
        state_ref[...] = chunk_decay * state + _dot_tn(kz, v)
        o_ref[sl, :] = o.astype(o_ref.dtype)


def _ret_chunk(qk, vg, log_gamma, *, batch, seq, dk, dv, qk_dim, v_dim):
    t = qk.shape[0]
    heads = qk_dim // dk
    step = _tile(seq, RET_STEP_ROWS)
    rows = _tile(step, CHUNK_ROWS)
    nsteps = seq // step
    kern = functools.partial(_ret_chunk_kernel, rows=rows, n_sub=step // rows)
    return pl.pallas_call(
        kern,
        grid=(batch, heads, nsteps),
        in_specs=[pl.BlockSpec(memory_space=pltpu.SMEM),
                  pl.BlockSpec((step, dk), lambda b, h, n: (b * nsteps + n, h)),
                  pl.BlockSpec((step, dk), lambda b, h, n: (b * nsteps + n, heads + h)),
                  pl.BlockSpec((step, dv), lambda b, h, n: (b * nsteps + n, h))],
        out_specs=pl.BlockSpec((step, dv), lambda b, h, n: (b * nsteps + n, h)),
        out_shape=jax.ShapeDtypeStruct((t, v_dim), BF16),
        scratch_shapes=[pltpu.VMEM((dk, dv), F32),
                        pltpu.VMEM((rows, rows), F32),
                        pltpu.VMEM((rows, 1), F32),
                        pltpu.VMEM((rows, 1), F32)],
        compiler_params=_params("parallel", "parallel", "arbitrary"),
        name="ret_chunk",
    )(log_gamma, qk, qk, vg)


def _residual_and_norm(y, w_ref, h_ref, ng_ref, out_refs, final):
    hn = h_ref[...] + _dot(y, w_ref[...])
    if final:
        out_refs[0][...] = _rms(hn, ng_ref[...])
    else:
        out_refs[0][...] = hn
        out_refs[1][...] = _rms(hn, ng_ref[...]).astype(out_refs[1].dtype)


def _ffn_down_kernel(y_ref, w_ref, h_ref, ng_ref, *out_refs, final):
    _residual_and_norm(y_ref[...], w_ref, h_ref, ng_ref, out_refs, final)


def _project_head_groups(build_group, groups, w_ref, h_ref, ng_ref, hout_ref, u_ref):
    gk = w_ref.shape[0] // groups
    acc = h_ref[...]
    for c in range(groups):
        acc = acc + _dot(build_group(c), w_ref[c * gk:(c + 1) * gk, :])
    hout_ref[...] = acc
    u_ref[...] = _rms(acc, ng_ref[...]).astype(u_ref.dtype)


def _ret_out_kernel(o_ref, g_ref, gn_ref, w_ref, h_ref, ng_ref, hout_ref, u_ref, *, heads):
    dv = o_ref.shape[1] // heads
    per = heads // RET_OUT_GROUPS

    def build_group(c):
        ys = []
        for hh in range(c * per, (c + 1) * per):
            sl = slice(hh * dv, (hh + 1) * dv)
            g = g_ref[:, sl].astype(F32)
            y = _head_norm(o_ref[:, sl].astype(F32), gn_ref[:, sl]) * (g * _sigmoid(g))
            ys.append(y.astype(BF16))
        return jnp.concatenate(ys, axis=-1)

    _project_head_groups(build_group, RET_OUT_GROUPS, w_ref, h_ref, ng_ref, hout_ref, u_ref)


def _ml_out_kernel(hid_ref, xc_ref, z_ref, gn_ref, skip_ref, w_ref, h_ref, ng_ref, hout_ref, u_ref,
                   *, heads):
    dh = hid_ref.shape[1] // heads
    per = heads // ML_OUT_GROUPS

    def build_group(c):
        ys = []
        for hh in range(c * per, (c + 1) * per):
            sl = slice(hh * dh, (hh + 1) * dh)
            z = z_ref[:, sl].astype(F32)
            hn = _head_norm(hid_ref[:, sl].astype(F32), gn_ref[:, sl])
            y = (hn + skip_ref[:, sl] * xc_ref[:, sl].astype(F32)) * (z * _sigmoid(z))
            ys.append(y.astype(BF16))
        return jnp.concatenate(ys, axis=-1)

    _project_head_groups(build_group, ML_OUT_GROUPS, w_ref, h_ref, ng_ref, hout_ref, u_ref)


def _sublayer_out(kern, acts, vecs, w, layer, h, ng, *, final=False, name):
    t, d = h.shape
    k = w.shape[1]
    tm = _tile(t, RESIDENT_ROWS)
    row_spec = pl.BlockSpec((tm, d), lambda i: (i, 0))
    vec_spec = pl.BlockSpec((1, k), lambda i: (0, 0))
    if final:
        out_specs, out_shape = row_spec, jax.ShapeDtypeStruct((t, d), F32)
    else:
        out_specs = [row_spec, row_spec]
        out_shape = [jax.ShapeDtypeStruct((t, d), F32), jax.ShapeDtypeStruct((t, d), BF16)]
    act_specs = [pl.BlockSpec((tm, k), functools.partial(lambda i, cb: (i, cb), cb=cb))
                 for _, cb in acts]
    return pl.pallas_call(
        kern,
        grid=(t // tm,),
        in_specs=act_specs + [vec_spec] * len(vecs) + [
            pl.BlockSpec((None, k, d), lambda i: (layer, 0, 0), pipeline_mode=pl.Buffered(1)),
            row_spec,
            pl.BlockSpec((1, d), lambda i: (0, 0))],
        out_specs=out_specs,
        out_shape=out_shape,
        compiler_params=_params("parallel"),
        name=name,
    )(*[a for a, _ in acts], *[v.reshape(1, k) for v in vecs], w, h, ng.reshape(1, d))


def _ffn_up_kernel(u_ref, wg_ref, wu_ref, o_ref):
    u = u_ref[...]
    a = _dot(u, wg_ref[...].astype(BF16))
    b = _dot(u, wu_ref[...].astype(BF16))
    o_ref[...] = (a * _sigmoid(a) * b).astype(o_ref.dtype)


def _ffn_up(u, wg, wu, layer):
    t, d = u.shape
    f = wg.shape[2]
    tm = _tile(t, STREAM_ROWS)
    tf = _tile(f, 512)
    return pl.pallas_call(
        _ffn_up_kernel,
        grid=(t // tm, f // tf),
        in_specs=[pl.BlockSpec((tm, d), lambda i, j: (i, 0)),
                  pl.BlockSpec((None, d, tf), lambda i, j: (layer, 0, j)),
                  pl.BlockSpec((None, d, tf), lambda i, j: (layer, 0, j))],
        out_specs=pl.BlockSpec((tm, tf), lambda i, j: (i, j)),
        out_shape=jax.ShapeDtypeStruct((t, f), BF16),
        compiler_params=_params("parallel", "arbitrary"),
        name="ffn_up",
    )(u, wg, wu)


def _ml_in_kernel(u_ref, wx_ref, wz_ref, cw_ref, cb_ref, bq_ref, bk_ref, bv_ref,
                  gq_ref, gk_ref, gv_ref, bg_ref,
                  q_ref, k_ref, v_ref, xc_ref, z_ref, gates_ref,
                  ext_ref, halo_ref, *, tm, tn, seq):
    i = pl.program_id(0)
    j = pl.program_id(1)
    hs = V7X_SUBLANES

    @pl.when(i == 0)
    def _():
        halo_ref[j] = jnp.zeros(halo_ref.shape[1:], F32)

    seq_start = (i * tm) % seq == 0
    ext_ref[0:hs, :] = jnp.where(seq_start, 0.0, halo_ref[j])
    rb = tm // ML_IN_ROW_BLOCKS
    wx = wx_ref[...].astype(BF16)
    wz = wz_ref[...].astype(BF16)
    for r in range(ML_IN_ROW_BLOCKS):
        rows = slice(r * rb, (r + 1) * rb)
        u = u_ref[rows, :]
        xm = _dot(u, wx)
        ext_ref[hs + r * rb:hs + (r + 1) * rb, :] = xm
        z_ref[rows, :] = _dot(u, wz).astype(z_ref.dtype)
        conv = cb_ref[...]
        for tap in range(CONV_WIDTH):
            off = hs - (CONV_WIDTH - 1) + tap + r * rb
            conv = conv + ext_ref[off:off + rb, :] * cw_ref[tap:tap + 1, :]
        xc = conv * _sigmoid(conv)
        xc_ref[rows, :] = xc.astype(xc_ref.dtype)
        gacc = jnp.where(j == 0, bg_ref[...], gates_ref[rows, :])
        for c in range(tn // V7X_MXU_DIM):
            sl = slice(c * V7X_MXU_DIM, (c + 1) * V7X_MXU_DIM)
            xcb = xc[:, sl].astype(BF16)
            xmb = xm[:, sl].astype(BF16)
            q = _dot(xcb, bq_ref[c]).astype(BF16)
            k = _dot(xcb, bk_ref[c]).astype(BF16)
            v = _dot(xmb, bv_ref[c]).astype(BF16)
            q_ref[rows, sl] = q
            k_ref[rows, sl] = k
            v_ref[rows, sl] = v
            gacc = gacc + _dot(q, gq_ref[sl, :]) + _dot(k, gk_ref[sl, :]) + _dot(v, gv_ref[sl, :])
        gates_ref[rows, :] = gacc
    halo_ref[j] = ext_ref[tm:tm + hs, :]


def _block_diag_tiles(w, tile):
    nb, bo, bi = w.shape
    per = tile // bi
    rows = jnp.swapaxes(w, 1, 2).reshape(nb // per, tile, bo)
    tiled = jnp.tile(rows, (1, 1, per))
    r_blk = lax.broadcasted_iota(jnp.int32, (tile, tile), 0) // bi
    c_blk = lax.broadcasted_iota(jnp.int32, (tile, tile), 1) // bo
    return jnp.where(r_blk == c_blk, tiled, 0.0)


def _ml_in(u, w_in, layer, conv_w, conv_b, w_q, w_k, w_v, w_gate, b_gate, *, seq):
    t, d = u.shape
    inner = w_in.shape[2] // 2
    tm = _tile(seq, 1024)
    tn = 512
    nj = inner // tn
    per_tile = tn // V7X_MXU_DIM
    ngates = w_gate.shape[1]

    def pad_gate(wg):
        return jnp.pad(wg, ((0, 0), (0, V7X_LANES - ngates))).astype(BF16)

    gq, gk, gv = (pad_gate(w_gate[s * inner:(s + 1) * inner]) for s in range(3))
    bg = jnp.pad(b_gate, (0, V7X_LANES - ngates)).reshape(1, V7X_LANES).astype(F32)
    bq, bk, bv = (_block_diag_tiles(w, V7X_MXU_DIM).astype(BF16) for w in (w_q, w_k, w_v))

    col = pl.BlockSpec((tm, tn), lambda i, j: (i, j))
    bd = pl.BlockSpec((per_tile, V7X_MXU_DIM, V7X_MXU_DIM), lambda i, j: (j, 0, 0))
    gw = pl.BlockSpec((tn, V7X_LANES), lambda i, j: (j, 0))
    act = jax.ShapeDtypeStruct((t, inner), BF16)
    kern = functools.partial(_ml_in_kernel, tm=tm, tn=tn, seq=seq)
    return pl.pallas_call(
        kern,
        grid=(t // tm, nj),
        in_specs=[pl.BlockSpec((tm, d), lambda i, j: (i, 0)),
                  pl.BlockSpec((None, d, tn), lambda i, j: (layer, 0, j)),
                  pl.BlockSpec((None, d, tn), lambda i, j: (layer, 0, nj + j)),
                  pl.BlockSpec((CONV_WIDTH, tn), lambda i, j: (0, j)),
                  pl.BlockSpec((1, tn), lambda i, j: (0, j)),
                  bd, bd, bd, gw, gw, gw,
                  pl.BlockSpec((1, V7X_LANES), lambda i, j: (0, 0))],
        out_specs=[col, col, col, col, col,
                   pl.BlockSpec((tm, V7X_LANES), lambda i, j: (i, 0))],
        out_shape=[act, act, act, act, act, jax.ShapeDtypeStruct((t, V7X_LANES), F32)],
        scratch_shapes=[pltpu.VMEM((tm + V7X_SUBLANES, tn), F32),
                        pltpu.VMEM((nj, V7X_SUBLANES, tn), F32)],
        compiler_params=_params("arbitrary", "arbitrary"),
        name="ml_in",
    )(u, w_in, w_in, conv_w, conv_b.reshape(1, inner), bq, bk, bv, gq, gk, gv, bg)


def _log_sigmoid(x):
    return jnp.minimum(x, 0.0) - jnp.log1p(jnp.exp(-jnp.abs(x)))


def _gate_prep_kernel(gates_ref, cum_ref, gates_t_ref, cum_t_ref, *, rows, n_sub, keep):
    ri = lax.broadcasted_iota(jnp.int32, (rows, rows), 0)
    ci = lax.broadcasted_iota(jnp.int32, (rows, rows), 1)
    tri = (ri >= ci).astype(BF16)
    for c in range(n_sub):
        sl = slice(c * rows, (c + 1) * rows)
        gates = gates_ref[sl, :]
        log_f = _log_sigmoid(gates)
        hi = log_f.astype(BF16)
        r1 = log_f - hi.astype(F32)
        mid = r1.astype(BF16)
        lo = (r1 - mid.astype(F32)).astype(BF16)
        cum = _dot(tri, hi) + _dot(tri, mid) + _dot(tri, lo)
        cum_ref[sl, :] = cum
        gates_t_ref[:, sl] = gates.T[:keep, :]
        cum_t_ref[:, sl] = cum.T[:keep, :]


def _gate_prep(gates, *, seq):
    t, lanes = gates.shape
    step = _tile(seq, 1024)
    rows = _tile(step, CHUNK_ROWS)
    keep = V7X_SUBLANES
    kern = functools.partial(_gate_prep_kernel, rows=rows, n_sub=step // rows, keep=keep)
    return pl.pallas_call(
        kern,
        grid=(t // step,),
        in_specs=[pl.BlockSpec((step, lanes), lambda i: (i, 0))],
        out_specs=[pl.BlockSpec((step, lanes), lambda i: (i, 0)),
                   pl.BlockSpec((keep, step), lambda i: (0, i)),
                   pl.BlockSpec((keep, step), lambda i: (0, i))],
        out_shape=[jax.ShapeDtypeStruct((t, lanes), F32),
                   jax.ShapeDtypeStruct((keep, t), F32),
                   jax.ShapeDtypeStruct((keep, t), F32)],
        compiler_params=_params("parallel"),
        name="gate_prep",
    )(gates)


def _ml_chunk_kernel(q_ref, k_ref, v_ref, gates_ref, cum_ref, gates_t_ref, cum_t_ref, o_ref,
                     c_ref, cb_ref, n_ref, m_ref, *, rows, n_sub, heads, k_scale):
    h = pl.program_id(1)

    @pl.when(pl.program_id(2) == 0)
    def _():
        c_ref[...] = jnp.zeros_like(c_ref)
        cb_ref[...] = jnp.zeros_like(cb_ref)
        n_ref[...] = jnp.zeros_like(n_ref)
        m_ref[...] = jnp.zeros_like(m_ref)

    lanes = gates_ref.shape[1]
    ri = lax.broadcasted_iota(jnp.int32, (rows, rows), 0)
    ci = lax.broadcasted_iota(jnp.int32, (rows, rows), 1)
    causal = ri >= ci
    lane = lax.broadcasted_iota(jnp.int32, (1, lanes), 1)

    for c in range(n_sub):
        sl = slice(c * rows, (c + 1) * rows)
        ig = jnp.sum(jnp.where(lane == h, gates_ref[sl, :], 0.0), axis=1, keepdims=True)
        bc = jnp.sum(jnp.where(lane == heads + h, cum_ref[sl, :], 0.0), axis=1, keepdims=True)
        ig_row = gates_t_ref[pl.ds(h, 1), sl]
        bc_row = cum_t_ref[pl.ds(heads + h, 1), sl]

        m_prev = m_ref[...]
        log_d = jnp.where(causal, bc - bc_row + ig_row, -jnp.inf)
        m_inter = bc + m_prev
        m_t = jnp.maximum(jnp.max(log_d, axis=1, keepdims=True), m_inter)
        dmat = jnp.exp(log_d - m_t)
        inter = jnp.exp(m_inter - m_t)

        q = q_ref[sl, :]
        k = k_ref[sl, :]
        v = v_ref[sl, :]
        scores = _dot_nt(q, k) * (dmat * k_scale)
        n_vec = n_ref[...]
        num = _dot(scores.astype(BF16), v) + inter * _dot(q, cb_ref[...])
        qn = jnp.sum(q.astype(F32) * n_vec, axis=1, keepdims=True)
        den = jnp.sum(scores, axis=1, keepdims=True) + inter * qn
        o_ref[sl, :] = (num / jnp.maximum(jnp.abs(den), jnp.exp(-m_t))).astype(o_ref.dtype)

        b_last = bc[rows - 1:rows, :]
        log_w = b_last - bc + ig
        m_new = jnp.maximum(b_last + m_prev, jnp.max(log_w, axis=0, keepdims=True))
        wts = jnp.exp(log_w - m_new) * k_scale
        decay = jnp.exp(b_last + m_prev - m_new)
        kw = k.astype(F32) * wts
        c_new = decay * c_ref[...] + _dot_tn(kw.astype(BF16), v)
        c_ref[...] = c_new
        cb_ref[...] = c_new.astype(BF16)
        n_ref[...] = decay * n_vec + jnp.sum(kw, axis=0, keepdims=True)
        m_ref[...] = m_new


def _ml_chunk(q, k, v, gates, cum, gates_t, cum_t, *, batch, seq, heads):
    t, inner = q.shape
    dh = inner // heads
    step = _tile(seq, ML_STEP_ROWS)
    rows = _tile(step, CHUNK_ROWS)
    nsteps = seq // step
    blk = pl.BlockSpec((step, dh), lambda b, h, n: (b * nsteps + n, h))
    col = pl.BlockSpec((step, gates.shape[1]), lambda b, h, n: (b * nsteps + n, 0))
    row = pl.BlockSpec((gates_t.shape[0], step), lambda b, h, n: (0, b * nsteps + n))
    kern = functools.partial(_ml_chunk_kernel, rows=rows, n_sub=step // rows, heads=heads,
                             k_scale=float(dh) ** -0.5)
    return pl.pallas_call(
        kern,
        grid=(batch, heads, nsteps),
        in_specs=[blk, blk, blk, col, col, row, row],
        out_specs=blk,
        out_shape=jax.ShapeDtypeStruct((t, inner), BF16),
        scratch_shapes=[pltpu.VMEM((dh, dh), F32),
                        pltpu.VMEM((dh, dh), BF16),
                        pltpu.VMEM((1, dh), F32),
                        pltpu.VMEM((1, 1), F32)],
        compiler_params=_params("parallel", "parallel", "arbitrary"),
        name="ml_chunk",
    )(q, k, v, gates, cum, gates_t, cum_t)


def kernel(x, positions, norm_mix_g, norm_ffn_g, ret_w_in, ret_gn_g, ret_w_out, ml_w_in, ml_conv_w, ml_conv_b, ml_w_q, ml_w_k, ml_w_v, ml_w_gate, ml_b_gate, ml_gn_g, ml_skip, ml_w_out, ffn_w_gate, ffn_w_up, ffn_w_down, final_g):
    batch, seq, d_model = x.shape
    t = batch * seq
    depth = norm_mix_g.shape[0]
    n_mixers = 2

    qk_dim = d_model
    v_dim = ret_w_out.shape[1]
    dk = qk_dim // RET_HEADS
    dv = v_dim // RET_HEADS

    inv_freq = ROPE_BASE ** (-jnp.arange(dk // 2, dtype=F32) * (2.0 / dk))
    cos, sin = _rope_table(positions.reshape(t), inv_freq)
    log_gamma = jnp.log(1.0 - 2.0 ** (-5.0 - jnp.arange(RET_HEADS, dtype=F32)))

    ret_w_out_bf = ret_w_out.astype(BF16)
    ml_w_out_bf = ml_w_out.astype(BF16)
    ffn_w_down_bf = ffn_w_down.astype(BF16)

    h = x.reshape(t, d_model)
    u = _rmsnorm(h, norm_mix_g[0])
    out = None
    for i in range(depth):
        j = i // n_mixers
        if i % n_mixers == 0:
            qk = _ret_qk(u, ret_w_in, j, cos, sin, dk=dk, qk_dim=qk_dim)
            vg = _matmul(u, ret_w_in, j, col_start=2 * qk_dim, col_count=2 * v_dim, name="ret_vg")
            o = _ret_chunk(qk, vg, log_gamma, batch=batch, seq=seq,
                           dk=dk, dv=dv, qk_dim=qk_dim, v_dim=v_dim)
            h, u = _sublayer_out(functools.partial(_ret_out_kernel, heads=RET_HEADS),
                                 [(o, 0), (vg, 1)], [ret_gn_g[j]],
                                 ret_w_out_bf, j, h, norm_ffn_g[i], name="ret_out")
        else:
            q, k, v, xc, z, gates = _ml_in(u, ml_w_in, j, ml_conv_w[j], ml_conv_b[j],
                                           ml_w_q[j], ml_w_k[j], ml_w_v[j], ml_w_gate[j],
                                           ml_b_gate[j], seq=seq)
            cum, gates_t, cum_t = _gate_prep(gates, seq=seq)
            hid = _ml_chunk(q, k, v, gates, cum, gates_t, cum_t, batch=batch, seq=seq, heads=ML_HEADS)
            h, u = _sublayer_out(functools.partial(_ml_out_kernel, heads=ML_HEADS),
                                 [(hid, 0), (xc, 0), (z, 0)], [ml_gn_g[j], ml_skip[j]],
                                 ml_w_out_bf, j, h, norm_ffn_g[i], name="ml_out")
        mid = _ffn_up(u, ffn_w_gate, ffn_w_up, i)
        if i + 1 < depth:
            h, u = _sublayer_out(functools.partial(_ffn_down_kernel, final=False), [(mid, 0)], [],
                                 ffn_w_down_bf, i, h, norm_mix_g[i + 1], name="ffn_down")
        else:
            out = _sublayer_out(functools.partial(_ffn_down_kernel, final=True), [(mid, 0)], [],
                                ffn_w_down_bf, i, h, final_g, final=True, name="ffn_down_final")
    return out.reshape(batch, seq, d_model)
```

```python
import functools

import jax
import jax.numpy as jnp
from jax import lax
from jax.experimental import pallas as pl
from jax.experimental.pallas import tpu as pltpu

F32 = jnp.float32
BF16 = jnp.bfloat16

EPS = 1e-6
ROPE_BASE = 10000.0
RET_HEADS = 8
ML_HEADS = 4
CONV_WIDTH = 4

V7X_VMEM_BYTES = 64 * 1024 * 1024
V7X_LANES = 128
V7X_SUBLANES = 8
V7X_MXU_DIM = 256

CHUNK_ROWS = V7X_MXU_DIM
RET_STEP_ROWS = 1024
ML_STEP_ROWS = 1024
STREAM_ROWS = 1024
MATMUL_ROWS = 2048
RESIDENT_ROWS = 256
RET_OUT_GROUPS = 8
ML_OUT_GROUPS = 4
ML_IN_ROW_BLOCKS = 4
VMEM_LIMIT = V7X_VMEM_BYTES - 8 * 1024 * 1024


def _tile(n, pref):
    t = min(pref, n)
    while n % t:
        t //= 2
    return t


def _params(*semantics):
    return pltpu.CompilerParams(dimension_semantics=semantics, vmem_limit_bytes=VMEM_LIMIT)


def _sigmoid(x):
    return 1.0 / (1.0 + jnp.exp(-x))


def _dot(a, b):
    return jnp.dot(a, b, preferred_element_type=F32)


def _dot_nt(a, b):
    return lax.dot_general(a, b, (((1,), (1,)), ((), ())), preferred_element_type=F32)


def _dot_tn(a, b):
    return lax.dot_general(a, b, (((0,), (0,)), ((), ())), preferred_element_type=F32)


def _rms(x, g):
    ms = jnp.mean(x * x, axis=-1, keepdims=True)
    return (x * g) * lax.rsqrt(ms + EPS)


def _head_norm(o, g):
    mu = jnp.mean(o, axis=-1, keepdims=True)
    d = o - mu
    var = jnp.mean(d * d, axis=-1, keepdims=True)
    return d * lax.rsqrt(var + EPS) * g


def _rmsnorm_kernel(x_ref, g_ref, o_ref):
    o_ref[...] = _rms(x_ref[...], g_ref[...]).astype(o_ref.dtype)


def _rmsnorm(x, g):
    t, d = x.shape
    tm = _tile(t, 512)
    return pl.pallas_call(
        _rmsnorm_kernel,
        grid=(t // tm,),
        in_specs=[pl.BlockSpec((tm, d), lambda i: (i, 0)),
                  pl.BlockSpec((1, d), lambda i: (0, 0))],
        out_specs=pl.BlockSpec((tm, d), lambda i: (i, 0)),
        out_shape=jax.ShapeDtypeStruct((t, d), BF16),
        compiler_params=_params("parallel"),
        name="rmsnorm",
    )(x, g.reshape(1, d))


def _rope_kernel(pos_ref, invf_ref, cos_ref, sin_ref):
    ang = pos_ref[...].astype(F32) * invf_ref[...]
    cos_ref[...] = jnp.cos(ang)
    sin_ref[...] = jnp.sin(ang)


def _rope_table(pos, inv_freq):
    t = pos.shape[0]
    half = inv_freq.shape[0]
    tm = _tile(t, 2048)
    return pl.pallas_call(
        _rope_kernel,
        grid=(t // tm,),
        in_specs=[pl.BlockSpec((tm, 1), lambda i: (i, 0)),
                  pl.BlockSpec((1, half), lambda i: (0, 0))],
        out_specs=[pl.BlockSpec((tm, half), lambda i: (i, 0))] * 2,
        out_shape=[jax.ShapeDtypeStruct((t, half), F32)] * 2,
        compiler_params=_params("parallel"),
        name="rope_table",
    )(pos.reshape(t, 1), inv_freq.reshape(1, half))


def _cast_weights_once(pairs):
    @pl.when(pl.program_id(1) == 0)
    def _():
        for w_ref, wb_ref in pairs:
            wb_ref[...] = w_ref[...].astype(BF16)


def _matmul_kernel(u_ref, w_ref, o_ref, wb_ref):
    _cast_weights_once([(w_ref, wb_ref)])
    o_ref[...] = _dot(u_ref[...], wb_ref[...]).astype(o_ref.dtype)


def _matmul(u, w, layer, *, col_start, col_count, name):
    t, d = u.shape
    tm = _tile(t, MATMUL_ROWS)
    tn = _tile(col_count, 1024)
    assert col_start % tn == 0
    j0 = col_start // tn
    return pl.pallas_call(
        _matmul_kernel,
        grid=(col_count // tn, t // tm),
        in_specs=[pl.BlockSpec((tm, d), lambda j, i: (i, 0)),
                  pl.BlockSpec((None, d, tn), lambda j, i: (layer, 0, j0 + j))],
        out_specs=pl.BlockSpec((tm, tn), lambda j, i: (i, j)),
        out_shape=jax.ShapeDtypeStruct((t, col_count), BF16),
        scratch_shapes=[pltpu.VMEM((d, tn), BF16)],
        compiler_params=_params("arbitrary", "arbitrary"),
        name=name,
    )(u, w)


def _ret_qk_kernel(u_ref, w_ref, cos_ref, sin_ref, o_ref, wb_ref, *, tn, dk, n_q_blocks, k_scale):
    _cast_weights_once([(w_ref, wb_ref)])
    acc = _dot(u_ref[...], wb_ref[...])
    scale = jnp.where(pl.program_id(0) >= n_q_blocks, k_scale, 1.0).astype(F32)
    c = cos_ref[...] * scale
    s = sin_ref[...] * scale
    half = dk // 2
    for hh in range(tn // dk):
        t1 = acc[:, hh * dk:hh * dk + half]
        t2 = acc[:, hh * dk + half:(hh + 1) * dk]
        o_ref[:, hh * dk:hh * dk + half] = (t1 * c - t2 * s).astype(o_ref.dtype)
        o_ref[:, hh * dk + half:(hh + 1) * dk] = (t1 * s + t2 * c).astype(o_ref.dtype)


def _ret_qk(u, w, layer, cos, sin, *, dk, qk_dim):
    t, d = u.shape
    tm = _tile(t, STREAM_ROWS)
    tn = _tile(qk_dim, 1024)
    kern = functools.partial(_ret_qk_kernel, tn=tn, dk=dk, n_q_blocks=qk_dim // tn,
                             k_scale=float(dk) ** -0.5)
    return pl.pallas_call(
        kern,
        grid=(2 * qk_dim // tn, t // tm),
        in_specs=[pl.BlockSpec((tm, d), lambda j, i: (i, 0)),
                  pl.BlockSpec((None, d, tn), lambda j, i: (layer, 0, j)),
                  pl.BlockSpec((tm, dk // 2), lambda j, i: (i, 0)),
                  pl.BlockSpec((tm, dk // 2), lambda j, i: (i, 0))],
        out_specs=pl.BlockSpec((tm, tn), lambda j, i: (i, j)),
        out_shape=jax.ShapeDtypeStruct((t, 2 * qk_dim), BF16),
        scratch_shapes=[pltpu.VMEM((d, tn), BF16)],
        compiler_params=_params("arbitrary", "arbitrary"),
        name="ret_qk",
    )(u, w, cos, sin)


def _ret_chunk_kernel(lg_ref, q_ref, k_ref, v_ref, o_ref, state_ref, dmat_ref, xi_ref, zeta_ref,
                      *, rows, n_sub):
    lg = lg_ref[pl.program_id(1)]

    @pl.when(pl.program_id(2) == 0)
    def _():
        state_ref[...] = jnp.zeros_like(state_ref)
        ri = lax.broadcasted_iota(jnp.int32, (rows, rows), 0)
        ci = lax.broadcasted_iota(jnp.int32, (rows, rows), 1)
        rel = (ri - ci).astype(F32)
        dmat_ref[...] = jnp.where(rel >= 0.0, jnp.exp(jnp.maximum(rel, 0.0) * lg), 0.0)
        idx = lax.broadcasted_iota(jnp.int32, (rows, 1), 0).astype(F32)
        xi_ref[...] = jnp.exp((idx + 1.0) * lg)
        zeta_ref[...] = jnp.exp((rows - 1.0 - idx) * lg)

    chunk_decay = jnp.exp(jnp.full((1, 1), float(rows), F32) * lg)
    for c in range(n_sub):
        sl = slice(c * rows, (c + 1) * rows)
        q = q_ref[sl, :]
        k = k_ref[sl, :]
        v = v_ref[sl, :]
        state = state_ref[...]
        scores = _dot_nt(q, k) * dmat_ref[...]
        o = _dot(scores.astype(BF16), v) + _dot(q, state.astype(BF16)) * xi_ref[...]
        kz = (k.astype(F32) * zeta_ref[...]).astype(BF16)
        state_ref[...] = chunk_decay * state + _dot_tn(kz, v)
        o_ref[sl, :] = o.astype(o_ref.dtype)


def _ret_chunk(qk, vg, log_gamma, *, batch, seq, dk, dv, qk_dim, v_dim):
    t = qk.shape[0]
    heads = qk_dim // dk
    step = _tile(seq, RET_STEP_ROWS)
    rows = _tile(step, CHUNK_ROWS)
    nsteps = seq // step
    kern = functools.partial(_ret_chunk_kernel, rows=rows, n_sub=step // rows)
    return pl.pallas_call(
        kern,
        grid=(batch, heads, nsteps),
        in_specs=[pl.BlockSpec(memory_space=pltpu.SMEM),
                  pl.BlockSpec((step, dk), lambda b, h, n: (b * nsteps + n, h)),
                  pl.BlockSpec((step, dk), lambda b, h, n: (b * nsteps + n, heads + h)),
                  pl.BlockSpec((step, dv), lambda b, h, n: (b * nsteps + n, h))],
        out_specs=pl.BlockSpec((step, dv), lambda b, h, n: (b * nsteps + n, h)),
        out_shape=jax.ShapeDtypeStruct((t, v_dim), BF16),
        scratch_shapes=[pltpu.VMEM((dk, dv), F32),
                        pltpu.VMEM((rows, rows), F32),
                        pltpu.VMEM((rows, 1), F32),
                        pltpu.VMEM((rows, 1), F32)],
        compiler_params=_params("parallel", "parallel", "arbitrary"),
        name="ret_chunk",
    )(log_gamma, qk, qk, vg)


def _residual_and_norm(y, w_ref, h_ref, ng_ref, out_refs, final):
    hn = h_ref[...] + _dot(y, w_ref[...])
    if final:
        out_refs[0][...] = _rms(hn, ng_ref[...])
    else:
        out_refs[0][...] = hn
        out_refs[1][...] = _rms(hn, ng_ref[...]).astype(out_refs[1].dtype)


def _ffn_down_kernel(y_ref, w_ref, h_ref, ng_ref, *out_refs, final):
    _residual_and_norm(y_ref[...], w_ref, h_ref, ng_ref, out_refs, final)


def _project_head_groups(build_group, groups, w_ref, h_ref, ng_ref, hout_ref, u_ref):
    gk = w_ref.shape[0] // groups
    acc = h_ref[...]
    for c in range(groups):
        acc = acc + _dot(build_group(c), w_ref[c * gk:(c + 1) * gk, :])
    hout_ref[...] = acc
    u_ref[...] = _rms(acc, ng_ref[...]).astype(u_ref.dtype)


def _ret_out_kernel(o_ref, g_ref, gn_ref, w_ref, h_ref, ng_ref, hout_ref, u_ref, *, heads):
    dv = o_ref.shape[1] // heads
    per = heads // RET_OUT_GROUPS

    def build_group(c):
        ys = []
        for hh in range(c * per, (c + 1) * per):
            sl = slice(hh * dv, (hh + 1) * dv)
            g = g_ref[:, sl].astype(F32)
            y = _head_norm(o_ref[:, sl].astype(F32), gn_ref[:, sl]) * (g * _sigmoid(g))
            ys.append(y.astype(BF16))
        return jnp.concatenate(ys, axis=-1)

    _project_head_groups(build_group, RET_OUT_GROUPS, w_ref, h_ref, ng_ref, hout_ref, u_ref)


def _ml_out_kernel(hid_ref, xc_ref, z_ref, gn_ref, skip_ref, w_ref, h_ref, ng_ref, hout_ref, u_ref,
                   *, heads):
    dh = hid_ref.shape[1] // heads
    per = heads // ML_OUT_GROUPS

    def build_group(c):
        ys = []
        for hh in range(c * per, (c + 1) * per):
            sl = slice(hh * dh, (hh + 1) * dh)
            z = z_ref[:, sl].astype(F32)
            hn = _head_norm(hid_ref[:, sl].astype(F32), gn_ref[:, sl])
            y = (hn + skip_ref[:, sl] * xc_ref[:, sl].astype(F32)) * (z * _sigmoid(z))
            ys.append(y.astype(BF16))
        return jnp.concatenate(ys, axis=-1)

    _project_head_groups(build_group, ML_OUT_GROUPS, w_ref, h_ref, ng_ref, hout_ref, u_ref)


def _sublayer_out(kern, acts, vecs, w, layer, h, ng, *, final=False, name):
    t, d = h.shape
    k = w.shape[1]
    tm = _tile(t, RESIDENT_ROWS)
    row_spec = pl.BlockSpec((tm, d), lambda i: (i, 0))
    vec_spec = pl.BlockSpec((1, k), lambda i: (0, 0))
    if final:
        out_specs, out_shape = row_spec, jax.ShapeDtypeStruct((t, d), F32)
    else:
        out_specs = [row_spec, row_spec]
        out_shape = [jax.ShapeDtypeStruct((t, d), F32), jax.ShapeDtypeStruct((t, d), BF16)]
    act_specs = [pl.BlockSpec((tm, k), functools.partial(lambda i, cb: (i, cb), cb=cb))
                 for _, cb in acts]
    return pl.pallas_call(
        kern,
        grid=(t // tm,),
        in_specs=act_specs + [vec_spec] * len(vecs) + [
            pl.BlockSpec((None, k, d), lambda i: (layer, 0, 0), pipeline_mode=pl.Buffered(1)),
            row_spec,
            pl.BlockSpec((1, d), lambda i: (0, 0))],
        out_specs=out_specs,
        out_shape=out_shape,
        compiler_params=_params("parallel"),
        name=name,
    )(*[a for a, _ in acts], *[v.reshape(1, k) for v in vecs], w, h, ng.reshape(1, d))


def _ffn_up_kernel(u_ref, wg_ref, wu_ref, o_ref, wgb_ref, wub_ref):
    _cast_weights_once([(wg_ref, wgb_ref), (wu_ref, wub_ref)])
    u = u_ref[...]
    a = _dot(u, wgb_ref[...])
    b = _dot(u, wub_ref[...])
    o_ref[...] = (a * _sigmoid(a) * b).astype(o_ref.dtype)


def _ffn_up(u, wg, wu, layer):
    t, d = u.shape
    f = wg.shape[2]
    tm = _tile(t, STREAM_ROWS)
    tf = _tile(f, 512)
    return pl.pallas_call(
        _ffn_up_kernel,
        grid=(f // tf, t // tm),
        in_specs=[pl.BlockSpec((tm, d), lambda j, i: (i, 0)),
                  pl.BlockSpec((None, d, tf), lambda j, i: (layer, 0, j)),
                  pl.BlockSpec((None, d, tf), lambda j, i: (layer, 0, j))],
        out_specs=pl.BlockSpec((tm, tf), lambda j, i: (i, j)),
        out_shape=jax.ShapeDtypeStruct((t, f), BF16),
        scratch_shapes=[pltpu.VMEM((d, tf), BF16), pltpu.VMEM((d, tf), BF16)],
        compiler_params=_params("arbitrary", "arbitrary"),
        name="ffn_up",
    )(u, wg, wu)


def _ml_in_kernel(u_ref, wx_ref, wz_ref, cw_ref, cb_ref, bq_ref, bk_ref, bv_ref,
                  gq_ref, gk_ref, gv_ref, bg_ref,
                  q_ref, k_ref, v_ref, xc_ref, z_ref, gates_ref,
                  ext_ref, halo_ref, *, tm, tn, seq):
    i = pl.program_id(0)
    j = pl.program_id(1)
    hs = V7X_SUBLANES

    @pl.when(i == 0)
    def _():
        halo_ref[j] = jnp.zeros(halo_ref.shape[1:], F32)

    seq_start = (i * tm) % seq == 0
    ext_ref[0:hs, :] = jnp.where(seq_start, 0.0, halo_ref[j])
    rb = tm // ML_IN_ROW_BLOCKS
    wx = wx_ref[...].astype(BF16)
    wz = wz_ref[...].astype(BF16)
    for r in range(ML_IN_ROW_BLOCKS):
        rows = slice(r * rb, (r + 1) * rb)
        u = u_ref[rows, :]
        xm = _dot(u, wx)
        ext_ref[hs + r * rb:hs + (r + 1) * rb, :] = xm
        z_ref[rows, :] = _dot(u, wz).astype(z_ref.dtype)
        conv = cb_ref[...]
        for tap in range(CONV_WIDTH):
            off = hs - (CONV_WIDTH - 1) + tap + r * rb
            conv = conv + ext_ref[off:off + rb, :] * cw_ref[tap:tap + 1, :]
        xc = conv * _sigmoid(conv)
        xc_ref[rows, :] = xc.astype(xc_ref.dtype)
        gacc = jnp.where(j == 0, bg_ref[...], gates_ref[rows, :])
        for c in range(tn // V7X_MXU_DIM):
            sl = slice(c * V7X_MXU_DIM, (c + 1) * V7X_MXU_DIM)
            xcb = xc[:, sl].astype(BF16)
            xmb = xm[:, sl].astype(BF16)
            q = _dot(xcb, bq_ref[c]).astype(BF16)
            k = _dot(xcb, bk_ref[c]).astype(BF16)
            v = _dot(xmb, bv_ref[c]).astype(BF16)
            q_ref[rows, sl] = q
            k_ref[rows, sl] = k
            v_ref[rows, sl] = v
            gacc = gacc + _dot(q, gq_ref[sl, :]) + _dot(k, gk_ref[sl, :]) + _dot(v, gv_ref[sl, :])
        gates_ref[rows, :] = gacc
    halo_ref[j] = ext_ref[tm:tm + hs, :]


def _block_diag_tiles(w, tile):
    nb, bo, bi = w.shape
    per = tile // bi
    rows = jnp.swapaxes(w, 1, 2).reshape(nb // per, tile, bo)
    tiled = jnp.tile(rows, (1, 1, per))
    r_blk = lax.broadcasted_iota(jnp.int32, (tile, tile), 0) // bi
    c_blk = lax.broadcasted_iota(jnp.int32, (tile, tile), 1) // bo
    return jnp.where(r_blk == c_blk, tiled, 0.0)


def _ml_in(u, w_in, layer, conv_w, conv_b, w_q, w_k, w_v, w_gate, b_gate, *, seq):
    t, d = u.shape
    inner = w_in.shape[2] // 2
    tm = _tile(seq, 1024)
    tn = 512
    nj = inner // tn
    per_tile = tn // V7X_MXU_DIM
    ngates = w_gate.shape[1]

    def pad_gate(wg):
        return jnp.pad(wg, ((0, 0), (0, V7X_LANES - ngates))).astype(BF16)

    gq, gk, gv = (pad_gate(w_gate[s * inner:(s + 1) * inner]) for s in range(3))
    bg = jnp.pad(b_gate, (0, V7X_LANES - ngates)).reshape(1, V7X_LANES).astype(F32)
    bq, bk, bv = (_block_diag_tiles(w, V7X_MXU_DIM).astype(BF16) for w in (w_q, w_k, w_v))

    col = pl.BlockSpec((tm, tn), lambda i, j: (i, j))
    bd = pl.BlockSpec((per_tile, V7X_MXU_DIM, V7X_MXU_DIM), lambda i, j: (j, 0, 0))
    gw = pl.BlockSpec((tn, V7X_LANES), lambda i, j: (j, 0))
    act = jax.ShapeDtypeStruct((t, inner), BF16)
    kern = functools.partial(_ml_in_kernel, tm=tm, tn=tn, seq=seq)
    return pl.pallas_call(
        kern,
        grid=(t // tm, nj),
        in_specs=[pl.BlockSpec((tm, d), lambda i, j: (i, 0)),
                  pl.BlockSpec((None, d, tn), lambda i, j: (layer, 0, j)),
                  pl.BlockSpec((None, d, tn), lambda i, j: (layer, 0, nj + j)),
                  pl.BlockSpec((CONV_WIDTH, tn), lambda i, j: (0, j)),
                  pl.BlockSpec((1, tn), lambda i, j: (0, j)),
                  bd, bd, bd, gw, gw, gw,
                  pl.BlockSpec((1, V7X_LANES), lambda i, j: (0, 0))],
        out_specs=[col, col, col, col, col,
                   pl.BlockSpec((tm, V7X_LANES), lambda i, j: (i, 0))],
        out_shape=[act, act, act, act, act, jax.ShapeDtypeStruct((t, V7X_LANES), F32)],
        scratch_shapes=[pltpu.VMEM((tm + V7X_SUBLANES, tn), F32),
                        pltpu.VMEM((nj, V7X_SUBLANES, tn), F32)],
        compiler_params=_params("arbitrary", "arbitrary"),
        name="ml_in",
    )(u, w_in, w_in, conv_w, conv_b.reshape(1, inner), bq, bk, bv, gq, gk, gv, bg)


def _log_sigmoid(x):
    return jnp.minimum(x, 0.0) - jnp.log1p(jnp.exp(-jnp.abs(x)))


def _gate_prep_kernel(gates_ref, cum_ref, gates_t_ref, cum_t_ref, *, rows, n_sub, keep):
    ri = lax.broadcasted_iota(jnp.int32, (rows, rows), 0)
    ci = lax.broadcasted_iota(jnp.int32, (rows, rows), 1)
    tri = (ri >= ci).astype(BF16)
    for c in range(n_sub):
        sl = slice(c * rows, (c + 1) * rows)
        gates = gates_ref[sl, :]
        log_f = _log_sigmoid(gates)
        hi = log_f.astype(BF16)
        r1 = log_f - hi.astype(F32)
        mid = r1.astype(BF16)
        lo = (r1 - mid.astype(F32)).astype(BF16)
        cum = _dot(tri, hi) + _dot(tri, mid) + _dot(tri, lo)
        cum_ref[sl, :] = cum
        gates_t_ref[:, sl] = gates.T[:keep, :]
        cum_t_ref[:, sl] = cum.T[:keep, :]


def _gate_prep(gates, *, seq):
    t, lanes = gates.shape
    step = _tile(seq, 1024)
    rows = _tile(step, CHUNK_ROWS)
    keep = V7X_SUBLANES
    kern = functools.partial(_gate_prep_kernel, rows=rows, n_sub=step // rows, keep=keep)
    return pl.pallas_call(
        kern,
        grid=(t // step,),
        in_specs=[pl.BlockSpec((step, lanes), lambda i: (i, 0))],
        out_specs=[pl.BlockSpec((step, lanes), lambda i: (i, 0)),
                   pl.BlockSpec((keep, step), lambda i: (0, i)),
                   pl.BlockSpec((keep, step), lambda i: (0, i))],
        out_shape=[jax.ShapeDtypeStruct((t, lanes), F32),
                   jax.ShapeDtypeStruct((keep, t), F32),
                   jax.ShapeDtypeStruct((keep, t), F32)],
        compiler_params=_params("parallel"),
        name="gate_prep",
    )(gates)


def _ml_chunk_kernel(q_ref, k_ref, v_ref, gates_ref, cum_ref, gates_t_ref, cum_t_ref, o_ref,
                     c_ref, cb_ref, n_ref, m_ref, *, rows, n_sub, heads, k_scale):
    h = pl.program_id(1)

    @pl.when(pl.program_id(2) == 0)
    def _():
        c_ref[...] = jnp.zeros_like(c_ref)
        cb_ref[...] = jnp.zeros_like(cb_ref)
        n_ref[...] = jnp.zeros_like(n_ref)
        m_ref[...] = jnp.zeros_like(m_ref)

    lanes = gates_ref.shape[1]
    ri = lax.broadcasted_iota(jnp.int32, (rows, rows), 0)
    ci = lax.broadcasted_iota(jnp.int32, (rows, rows), 1)
    causal = ri >= ci
    lane = lax.broadcasted_iota(jnp.int32, (1, lanes), 1)

    for c in range(n_sub):
        sl = slice(c * rows, (c + 1) * rows)
        ig = jnp.sum(jnp.where(lane == h, gates_ref[sl, :], 0.0), axis=1, keepdims=True)
        bc = jnp.sum(jnp.where(lane == heads + h, cum_ref[sl, :], 0.0), axis=1, keepdims=True)
        ig_row = gates_t_ref[pl.ds(h, 1), sl]
        bc_row = cum_t_ref[pl.ds(heads + h, 1), sl]

        m_prev = m_ref[...]
        log_d = jnp.where(causal, bc - bc_row + ig_row, -jnp.inf)
        m_inter = bc + m_prev
        m_t = jnp.maximum(jnp.max(log_d, axis=1, keepdims=True), m_inter)
        dmat = jnp.exp(log_d - m_t)
        inter = jnp.exp(m_inter - m_t)

        q = q_ref[sl, :]
        k = k_ref[sl, :]
        v = v_ref[sl, :]
        scores = _dot_nt(q, k) * (dmat * k_scale)
        n_vec = n_ref[...]
        num = _dot(scores.astype(BF16), v) + inter * _dot(q, cb_ref[...])
        qn = jnp.sum(q.astype(F32) * n_vec, axis=1, keepdims=True)
        den = jnp.sum(scores, axis=1, keepdims=True) + inter * qn
        o_ref[sl, :] = (num / jnp.maximum(jnp.abs(den), jnp.exp(-m_t))).astype(o_ref.dtype)

        b_last = bc[rows - 1:rows, :]
        log_w = b_last - bc + ig
        m_new = jnp.maximum(b_last + m_prev, jnp.max(log_w, axis=0, keepdims=True))
        wts = jnp.exp(log_w - m_new) * k_scale
        decay = jnp.exp(b_last + m_prev - m_new)
        kw = k.astype(F32) * wts
        c_new = decay * c_ref[...] + _dot_tn(kw.astype(BF16), v)
        c_ref[...] = c_new
        cb_ref[...] = c_new.astype(BF16)
        n_ref[...] = decay * n_vec + jnp.sum(kw, axis=0, keepdims=True)
        m_ref[...] = m_new


def _ml_chunk(q, k, v, gates, cum, gates_t, cum_t, *, batch, seq, heads):
    t, inner = q.shape
    dh = inner // heads
    step = _tile(seq, ML_STEP_ROWS)
    rows = _tile(step, CHUNK_ROWS)
    nsteps = seq // step
    blk = pl.BlockSpec((step, dh), lambda b, h, n: (b * nsteps + n, h))
    col = pl.BlockSpec((step, gates.shape[1]), lambda b, h, n: (b * nsteps + n, 0))
    row = pl.BlockSpec((gates_t.shape[0], step), lambda b, h, n: (0, b * nsteps + n))
    kern = functools.partial(_ml_chunk_kernel, rows=rows, n_sub=step // rows, heads=heads,
                             k_scale=float(dh) ** -0.5)
    return pl.pallas_call(
        kern,
        grid=(batch, heads, nsteps),
        in_specs=[blk, blk, blk, col, col, row, row],
        out_specs=blk,
        out_shape=jax.ShapeDtypeStruct((t, inner), BF16),
        scratch_shapes=[pltpu.VMEM((dh, dh), F32),
                        pltpu.VMEM((dh, dh), BF16),
                        pltpu.VMEM((1, dh), F32),
                        pltpu.VMEM((1, 1), F32)],
        compiler_params=_params("parallel", "parallel", "arbitrary"),
        name="ml_chunk",
    )(q, k, v, gates, cum, gates_t, cum_t)


def kernel(x, positions, norm_mix_g, norm_ffn_g, ret_w_in, ret_gn_g, ret_w_out, ml_w_in, ml_conv_w, ml_conv_b, ml_w_q, ml_w_k, ml_w_v, ml_w_gate, ml_b_gate, ml_gn_g, ml_skip, ml_w_out, ffn_w_gate, ffn_w_up, ffn_w_down, final_g):
    batch, seq, d_model = x.shape
    t = batch * seq
    depth = norm_mix_g.shape[0]
    n_mixers = 2

    qk_dim = d_model
    v_dim = ret_w_out.shape[1]
    dk = qk_dim // RET_HEADS
    dv = v_dim // RET_HEADS

    inv_freq = ROPE_BASE ** (-jnp.arange(dk // 2, dtype=F32) * (2.0 / dk))
    cos, sin = _rope_table(positions.reshape(t), inv_freq)
    log_gamma = jnp.log(1.0 - 2.0 ** (-5.0 - jnp.arange(RET_HEADS, dtype=F32)))

    ret_w_out_bf = ret_w_out.astype(BF16)
    ml_w_out_bf = ml_w_out.astype(BF16)
    ffn_w_down_bf = ffn_w_down.astype(BF16)

    h = x.reshape(t, d_model)
    u = _rmsnorm(h, norm_mix_g[0])
    out = None
    for i in range(depth):
        j = i // n_mixers
        if i % n_mixers == 0:
            qk = _ret_qk(u, ret_w_in, j, cos, sin, dk=dk, qk_dim=qk_dim)
            vg = _matmul(u, ret_w_in, j, col_start=2 * qk_dim, col_count=2 * v_dim, name="ret_vg")
            o = _ret_chunk(qk, vg, log_gamma, batch=batch, seq=seq,
                           dk=dk, dv=dv, qk_dim=qk_dim, v_dim=v_dim)
            h, u = _sublayer_out(functools.partial(_ret_out_kernel, heads=RET_HEADS),
                                 [(o, 0), (vg, 1)], [ret_gn_g[j]],
                                 ret_w_out_bf, j, h, norm_ffn_g[i], name="ret_out")
        else:
            q, k, v, xc, z, gates = _ml_in(u, ml_w_in, j, ml_conv_w[j], ml_conv_b[j],
                                           ml_w_q[j], ml_w_k[j], ml_w_v[j], ml_w_gate[j],
                                           ml_b_gate[j], seq=seq)
            cum, gates_t, cum_t = _gate_prep(gates, seq=seq)
            hid = _ml_chunk(q, k, v, gates, cum, gates_t, cum_t, batch=batch, seq=seq, heads=ML_HEADS)
            h, u = _sublayer_out(functools.partial(_ml_out_kernel, heads=ML_HEADS),
                                 [(hid, 0), (xc, 0), (z, 0)], [ml_gn_g[j], ml_skip[j]],
                                 ml_w_out_bf, j, h, norm_ffn_g[i], name="ml_out")
        mid = _ffn_up(u, ffn_w_gate, ffn_w_up, i)
        if i + 1 < depth:
            h, u = _sublayer_out(functools.partial(_ffn_down_kernel, final=False), [(mid, 0)], [],
                                 ffn_w_down_bf, i, h, norm_mix_g[i + 1], name="ffn_down")
        else:
            out = _sublayer_out(functools.partial(_ffn_down_kernel, final=True), [(mid, 0)], [],
                                ffn_w_down_bf, i, h, final_g, final=True, name="ffn_down_final")
    return out.reshape(batch, seq, d_model)
```

```python
import functools

import jax
import jax.numpy as jnp
from jax import lax
from jax.experimental import pallas as pl
from jax.experimental.pallas import tpu as pltpu

F32 = jnp.float32
BF16 = jnp.bfloat16

EPS = 1e-6
ROPE_BASE = 10000.0
RET_HEADS = 8
ML_HEADS = 4
CONV_WIDTH = 4

V7X_VMEM_BYTES = 64 * 1024 * 1024
V7X_LANES = 128
V7X_SUBLANES = 8
V7X_MXU_DIM = 256

CHUNK_ROWS = V7X_MXU_DIM
RET_STEP_ROWS = 1024
ML_STEP_ROWS = 1024
STREAM_ROWS = 1024
MATMUL_ROWS = 2048
RESIDENT_ROWS = 256
RET_OUT_GROUPS = 8
ML_OUT_GROUPS = 4
ML_IN_ROW_BLOCKS = 4
VMEM_LIMIT = V7X_VMEM_BYTES - 8 * 1024 * 1024


def _tile(n, pref):
    t = min(pref, n)
    while n % t:
        t //= 2
    return t


def _params(*semantics):
    return pltpu.CompilerParams(dimension_semantics=semantics, vmem_limit_bytes=VMEM_LIMIT)


def _sigmoid(x):
    return 1.0 / (1.0 + jnp.exp(-x))


def _dot(a, b):
    return jnp.dot(a, b, preferred_element_type=F32)


def _dot_nt(a, b):
    return lax.dot_general(a, b, (((1,), (1,)), ((), ())), preferred_element_type=F32)


def _dot_tn(a, b):
    return lax.dot_general(a, b, (((0,), (0,)), ((), ())), preferred_element_type=F32)


def _rms(x, g):
    ms = jnp.mean(x * x, axis=-1, keepdims=True)
    return (x * g) * lax.rsqrt(ms + EPS)


def _head_norm(o, g):
    mu = jnp.mean(o, axis=-1, keepdims=True)
    d = o - mu
    var = jnp.mean(d * d, axis=-1, keepdims=True)
    return d * lax.rsqrt(var + EPS) * g


def _rmsnorm_kernel(x_ref, g_ref, o_ref):
    o_ref[...] = _rms(x_ref[...], g_ref[...]).astype(o_ref.dtype)


def _rmsnorm(x, g):
    t, d = x.shape
    tm = _tile(t, 512)
    return pl.pallas_call(
        _rmsnorm_kernel,
        grid=(t // tm,),
        in_specs=[pl.BlockSpec((tm, d), lambda i: (i, 0)),
                  pl.BlockSpec((1, d), lambda i: (0, 0))],
        out_specs=pl.BlockSpec((tm, d), lambda i: (i, 0)),
        out_shape=jax.ShapeDtypeStruct((t, d), BF16),
        compiler_params=_params("parallel"),
        name="rmsnorm",
    )(x, g.reshape(1, d))


def _rope_kernel(pos_ref, invf_ref, cos_ref, sin_ref):
    ang = pos_ref[...].astype(F32) * invf_ref[...]
    cos_ref[...] = jnp.cos(ang)
    sin_ref[...] = jnp.sin(ang)


def _rope_table(pos, inv_freq):
    t = pos.shape[0]
    half = inv_freq.shape[0]
    tm = _tile(t, 2048)
    return pl.pallas_call(
        _rope_kernel,
        grid=(t // tm,),
        in_specs=[pl.BlockSpec((tm, 1), lambda i: (i, 0)),
                  pl.BlockSpec((1, half), lambda i: (0, 0))],
        out_specs=[pl.BlockSpec((tm, half), lambda i: (i, 0))] * 2,
        out_shape=[jax.ShapeDtypeStruct((t, half), F32)] * 2,
        compiler_params=_params("parallel"),
        name="rope_table",
    )(pos.reshape(t, 1), inv_freq.reshape(1, half))


def _cast_weights_once(pairs):
    @pl.when(pl.program_id(1) == 0)
    def _():
        for w_ref, wb_ref in pairs:
            wb_ref[...] = w_ref[...].astype(BF16)


def _store_heads(o_ref, acc):
    hd = o_ref.shape[2]
    for hh in range(o_ref.shape[0]):
        o_ref[hh] = acc[:, hh * hd:(hh + 1) * hd].astype(o_ref.dtype)


def _matmul_kernel(u_ref, w_ref, o_ref, wb_ref, *, head_major):
    _cast_weights_once([(w_ref, wb_ref)])
    acc = _dot(u_ref[...], wb_ref[...])
    if head_major:
        _store_heads(o_ref, acc)
    else:
        o_ref[...] = acc.astype(o_ref.dtype)


def _matmul(u, w, layer, *, col_start, col_count, head_dim=None, name):
    t, d = u.shape
    tm = _tile(t, MATMUL_ROWS)
    tn = _tile(col_count, 1024)
    assert col_start % tn == 0
    j0 = col_start // tn
    if head_dim is None:
        out_spec = pl.BlockSpec((tm, tn), lambda j, i: (i, j))
        out_shape = jax.ShapeDtypeStruct((t, col_count), BF16)
    else:
        out_spec = pl.BlockSpec((tn // head_dim, tm, head_dim), lambda j, i: (j, i, 0))
        out_shape = jax.ShapeDtypeStruct((col_count // head_dim, t, head_dim), BF16)
    return pl.pallas_call(
        functools.partial(_matmul_kernel, head_major=head_dim is not None),
        grid=(col_count // tn, t // tm),
        in_specs=[pl.BlockSpec((tm, d), lambda j, i: (i, 0)),
                  pl.BlockSpec((None, d, tn), lambda j, i: (layer, 0, j0 + j))],
        out_specs=out_spec,
        out_shape=out_shape,
        scratch_shapes=[pltpu.VMEM((d, tn), BF16)],
        compiler_params=_params("arbitrary", "arbitrary"),
        name=name,
    )(u, w)


def _ret_qk_kernel(u_ref, w_ref, cos_ref, sin_ref, o_ref, wb_ref, *, tn, dk, n_q_blocks, k_scale):
    _cast_weights_once([(w_ref, wb_ref)])
    acc = _dot(u_ref[...], wb_ref[...])
    scale = jnp.where(pl.program_id(0) >= n_q_blocks, k_scale, 1.0).astype(F32)
    c = cos_ref[...] * scale
    s = sin_ref[...] * scale
    half = dk // 2
    for hh in range(tn // dk):
        t1 = acc[:, hh * dk:hh * dk + half]
        t2 = acc[:, hh * dk + half:(hh + 1) * dk]
        o_ref[hh, :, :half] = (t1 * c - t2 * s).astype(o_ref.dtype)
        o_ref[hh, :, half:] = (t1 * s + t2 * c).astype(o_ref.dtype)


def _ret_qk(u, w, layer, cos, sin, *, dk, qk_dim):
    t, d = u.shape
    tm = _tile(t, STREAM_ROWS)
    tn = _tile(qk_dim, 1024)
    kern = functools.partial(_ret_qk_kernel, tn=tn, dk=dk, n_q_blocks=qk_dim // tn,
                             k_scale=float(dk) ** -0.5)
    return pl.pallas_call(
        kern,
        grid=(2 * qk_dim // tn, t // tm),
        in_specs=[pl.BlockSpec((tm, d), lambda j, i: (i, 0)),
                  pl.BlockSpec((None, d, tn), lambda j, i: (layer, 0, j)),
                  pl.BlockSpec((tm, dk // 2), lambda j, i: (i, 0)),
                  pl.BlockSpec((tm, dk // 2), lambda j, i: (i, 0))],
        out_specs=pl.BlockSpec((tn // dk, tm, dk), lambda j, i: (j, i, 0)),
        out_shape=jax.ShapeDtypeStruct((2 * qk_dim // dk, t, dk), BF16),
        scratch_shapes=[pltpu.VMEM((d, tn), BF16)],
        compiler_params=_params("arbitrary", "arbitrary"),
        name="ret_qk",
    )(u, w, cos, sin)


def _ret_chunk_kernel(lg_ref, q_ref, k_ref, v_ref, o_ref, state_ref, dmat_ref, xi_ref, zeta_ref,
                      *, rows, n_sub):
    lg = lg_ref[pl.program_id(1)]

    @pl.when(pl.program_id(2) == 0)
    def _():
        state_ref[...] = jnp.zeros_like(state_ref)
        ri = lax.broadcasted_iota(jnp.int32, (rows, rows), 0)
        ci = lax.broadcasted_iota(jnp.int32, (rows, rows), 1)
        rel = (ri - ci).astype(F32)
        dmat_ref[...] = jnp.where(rel >= 0.0, jnp.exp(jnp.maximum(rel, 0.0) * lg), 0.0)
        idx = lax.broadcasted_iota(jnp.int32, (rows, 1), 0).astype(F32)
        xi_ref[...] = jnp.exp((idx + 1.0) * lg)
        zeta_ref[...] = jnp.exp((rows - 1.0 - idx) * lg)

    chunk_decay = jnp.exp(jnp.full((1, 1), float(rows), F32) * lg)
    for c in range(n_sub):
        sl = slice(c * rows, (c + 1) * rows)
        q = q_ref[sl, :]
        k = k_ref[sl, :]
        v = v_ref[sl, :]
        state = state_ref[...]
        scores = _dot_nt(q, k) * dmat_ref[...]
        o = _dot(scores.astype(BF16), v) + _dot(q, state.astype(BF16)) * xi_ref[...]
        kz = (k.astype(F32) * zeta_ref[...]).astype(BF16)
        state_ref[...] = chunk_decay * state + _dot_tn(kz, v)
        o_ref[sl, :] = o.astype(o_ref.dtype)


def _ret_chunk(qk, v, log_gamma, *, batch, seq):
    heads, t, dv = v.shape
    dk = qk.shape[2]
    step = _tile(seq, RET_STEP_ROWS)
    rows = _tile(step, CHUNK_ROWS)
    nsteps = seq // step
    kern = functools.partial(_ret_chunk_kernel, rows=rows, n_sub=step // rows)
    return pl.pallas_call(
        kern,
        grid=(batch, heads, nsteps),
        in_specs=[pl.BlockSpec(memory_space=pltpu.SMEM),
                  pl.BlockSpec((None, step, dk), lambda b, h, n: (h, b * nsteps + n, 0)),
                  pl.BlockSpec((None, step, dk), lambda b, h, n: (heads + h, b * nsteps + n, 0)),
                  pl.BlockSpec((None, step, dv), lambda b, h, n: (h, b * nsteps + n, 0))],
        out_specs=pl.BlockSpec((None, step, dv), lambda b, h, n: (h, b * nsteps + n, 0)),
        out_shape=jax.ShapeDtypeStruct((heads, t, dv), BF16),
        scratch_shapes=[pltpu.VMEM((dk, dv), F32),
                        pltpu.VMEM((rows, rows), F32),
                        pltpu.VMEM((rows, 1), F32),
                        pltpu.VMEM((rows, 1), F32)],
        compiler_params=_params("parallel", "parallel", "arbitrary"),
        name="ret_chunk",
    )(log_gamma, qk, qk, v)


def _residual_and_norm(y, w_ref, h_ref, ng_ref, out_refs, final):
    hn = h_ref[...] + _dot(y, w_ref[...])
    if final:
        out_refs[0][...] = _rms(hn, ng_ref[...])
    else:
        out_refs[0][...] = hn
        out_refs[1][...] = _rms(hn, ng_ref[...]).astype(out_refs[1].dtype)


def _ffn_down_kernel(y_ref, w_ref, h_ref, ng_ref, *out_refs, final):
    _residual_and_norm(y_ref[...], w_ref, h_ref, ng_ref, out_refs, final)


def _project_head_groups(build_group, groups, w_ref, h_ref, ng_ref, hout_ref, u_ref):
    gk = w_ref.shape[0] // groups
    acc = h_ref[...]
    for c in range(groups):
        acc = acc + _dot(build_group(c), w_ref[c * gk:(c + 1) * gk, :])
    hout_ref[...] = acc
    u_ref[...] = _rms(acc, ng_ref[...]).astype(u_ref.dtype)


def _ret_out_kernel(o_ref, g_ref, gn_ref, w_ref, h_ref, ng_ref, hout_ref, u_ref):
    heads, _, dv = o_ref.shape
    per = heads // RET_OUT_GROUPS

    def build_group(c):
        ys = []
        for hh in range(c * per, (c + 1) * per):
            sl = slice(hh * dv, (hh + 1) * dv)
            g = g_ref[:, sl].astype(F32)
            y = _head_norm(o_ref[hh].astype(F32), gn_ref[:, sl]) * (g * _sigmoid(g))
            ys.append(y.astype(BF16))
        return jnp.concatenate(ys, axis=-1)

    _project_head_groups(build_group, RET_OUT_GROUPS, w_ref, h_ref, ng_ref, hout_ref, u_ref)


def _ml_out_kernel(hid_ref, xc_ref, z_ref, gn_ref, skip_ref, w_ref, h_ref, ng_ref, hout_ref, u_ref,
                   *, heads):
    dh = hid_ref.shape[1] // heads
    per = heads // ML_OUT_GROUPS

    def build_group(c):
        ys = []
        for hh in range(c * per, (c + 1) * per):
            sl = slice(hh * dh, (hh + 1) * dh)
            z = z_ref[:, sl].astype(F32)
            hn = _head_norm(hid_ref[:, sl].astype(F32), gn_ref[:, sl])
            y = (hn + skip_ref[:, sl] * xc_ref[:, sl].astype(F32)) * (z * _sigmoid(z))
            ys.append(y.astype(BF16))
        return jnp.concatenate(ys, axis=-1)

    _project_head_groups(build_group, ML_OUT_GROUPS, w_ref, h_ref, ng_ref, hout_ref, u_ref)


def _sublayer_out(kern, acts, vecs, w, layer, h, ng, *, final=False, name):
    t, d = h.shape
    k = w.shape[1]
    tm = _tile(t, RESIDENT_ROWS)
    row_spec = pl.BlockSpec((tm, d), lambda i: (i, 0))
    vec_spec = pl.BlockSpec((1, k), lambda i: (0, 0))
    if final:
        out_specs, out_shape = row_spec, jax.ShapeDtypeStruct((t, d), F32)
    else:
        out_specs = [row_spec, row_spec]
        out_shape = [jax.ShapeDtypeStruct((t, d), F32), jax.ShapeDtypeStruct((t, d), BF16)]
    act_specs = [pl.BlockSpec((tm, k), functools.partial(lambda i, cb: (i, cb), cb=cb)) if a.ndim == 2
                 else pl.BlockSpec((a.shape[0], tm, a.shape[2]), lambda i: (0, i, 0))
                 for a, cb in acts]
    return pl.pallas_call(
        kern,
        grid=(t // tm,),
        in_specs=act_specs + [vec_spec] * len(vecs) + [
            pl.BlockSpec((None, k, d), lambda i: (layer, 0, 0), pipeline_mode=pl.Buffered(1)),
            row_spec,
            pl.BlockSpec((1, d), lambda i: (0, 0))],
        out_specs=out_specs,
        out_shape=out_shape,
        compiler_params=_params("parallel"),
        name=name,
    )(*[a for a, _ in acts], *[v.reshape(1, k) for v in vecs], w, h, ng.reshape(1, d))


def _ffn_up_kernel(u_ref, wg_ref, wu_ref, o_ref, wgb_ref, wub_ref):
    _cast_weights_once([(wg_ref, wgb_ref), (wu_ref, wub_ref)])
    u = u_ref[...]
    a = _dot(u, wgb_ref[...])
    b = _dot(u, wub_ref[...])
    o_ref[...] = (a * _sigmoid(a) * b).astype(o_ref.dtype)


def _ffn_up(u, wg, wu, layer):
    t, d = u.shape
    f = wg.shape[2]
    tm = _tile(t, STREAM_ROWS)
    tf = _tile(f, 512)
    return pl.pallas_call(
        _ffn_up_kernel,
        grid=(f // tf, t // tm),
        in_specs=[pl.BlockSpec((tm, d), lambda j, i: (i, 0)),
                  pl.BlockSpec((None, d, tf), lambda j, i: (layer, 0, j)),
                  pl.BlockSpec((None, d, tf), lambda j, i: (layer, 0, j))],
        out_specs=pl.BlockSpec((tm, tf), lambda j, i: (i, j)),
        out_shape=jax.ShapeDtypeStruct((t, f), BF16),
        scratch_shapes=[pltpu.VMEM((d, tf), BF16), pltpu.VMEM((d, tf), BF16)],
        compiler_params=_params("arbitrary", "arbitrary"),
        name="ffn_up",
    )(u, wg, wu)


def _ml_in_kernel(u_ref, wx_ref, wz_ref, cw_ref, cb_ref, bq_ref, bk_ref, bv_ref,
                  gq_ref, gk_ref, gv_ref, bg_ref,
                  q_ref, k_ref, v_ref, xc_ref, z_ref, gates_ref,
                  ext_ref, halo_ref, *, tm, tn, seq):
    i = pl.program_id(0)
    j = pl.program_id(1)
    hs = V7X_SUBLANES

    @pl.when(i == 0)
    def _():
        halo_ref[j] = jnp.zeros(halo_ref.shape[1:], F32)

    seq_start = (i * tm) % seq == 0
    ext_ref[0:hs, :] = jnp.where(seq_start, 0.0, halo_ref[j])
    rb = tm // ML_IN_ROW_BLOCKS
    wx = wx_ref[...].astype(BF16)
    wz = wz_ref[...].astype(BF16)
    for r in range(ML_IN_ROW_BLOCKS):
        rows = slice(r * rb, (r + 1) * rb)
        u = u_ref[rows, :]
        xm = _dot(u, wx)
        ext_ref[hs + r * rb:hs + (r + 1) * rb, :] = xm
        z_ref[rows, :] = _dot(u, wz).astype(z_ref.dtype)
        conv = cb_ref[...]
        for tap in range(CONV_WIDTH):
            off = hs - (CONV_WIDTH - 1) + tap + r * rb
            conv = conv + ext_ref[off:off + rb, :] * cw_ref[tap:tap + 1, :]
        xc = conv * _sigmoid(conv)
        xc_ref[rows, :] = xc.astype(xc_ref.dtype)
        gacc = jnp.where(j == 0, bg_ref[...], gates_ref[rows, :])
        for c in range(tn // V7X_MXU_DIM):
            sl = slice(c * V7X_MXU_DIM, (c + 1) * V7X_MXU_DIM)
            xcb = xc[:, sl].astype(BF16)
            xmb = xm[:, sl].astype(BF16)
            q = _dot(xcb, bq_ref[c]).astype(BF16)
            k = _dot(xcb, bk_ref[c]).astype(BF16)
            v = _dot(xmb, bv_ref[c]).astype(BF16)
            q_ref[rows, sl] = q
            k_ref[rows, sl] = k
            v_ref[rows, sl] = v
            gacc = gacc + _dot(q, gq_ref[sl, :]) + _dot(k, gk_ref[sl, :]) + _dot(v, gv_ref[sl, :])
        gates_ref[rows, :] = gacc
    halo_ref[j] = ext_ref[tm:tm + hs, :]


def _block_diag_tiles(w, tile):
    nb, bo, bi = w.shape
    per = tile // bi
    rows = jnp.swapaxes(w, 1, 2).reshape(nb // per, tile, bo)
    tiled = jnp.tile(rows, (1, 1, per))
    r_blk = lax.broadcasted_iota(jnp.int32, (tile, tile), 0) // bi
    c_blk = lax.broadcasted_iota(jnp.int32, (tile, tile), 1) // bo
    return jnp.where(r_blk == c_blk, tiled, 0.0)


def _ml_in(u, w_in, layer, conv_w, conv_b, w_q, w_k, w_v, w_gate, b_gate, *, seq):
    t, d = u.shape
    inner = w_in.shape[2] // 2
    tm = _tile(seq, 1024)
    tn = 512
    nj = inner // tn
    per_tile = tn // V7X_MXU_DIM
    ngates = w_gate.shape[1]

    def pad_gate(wg):
        return jnp.pad(wg, ((0, 0), (0, V7X_LANES - ngates))).astype(BF16)

    gq, gk, gv = (pad_gate(w_gate[s * inner:(s + 1) * inner]) for s in range(3))
    bg = jnp.pad(b_gate, (0, V7X_LANES - ngates)).reshape(1, V7X_LANES).astype(F32)
    bq, bk, bv = (_block_diag_tiles(w, V7X_MXU_DIM).astype(BF16) for w in (w_q, w_k, w_v))

    col = pl.BlockSpec((tm, tn), lambda i, j: (i, j))
    bd = pl.BlockSpec((per_tile, V7X_MXU_DIM, V7X_MXU_DIM), lambda i, j: (j, 0, 0))
    gw = pl.BlockSpec((tn, V7X_LANES), lambda i, j: (j, 0))
    act = jax.ShapeDtypeStruct((t, inner), BF16)
    kern = functools.partial(_ml_in_kernel, tm=tm, tn=tn, seq=seq)
    return pl.pallas_call(
        kern,
        grid=(t // tm, nj),
        in_specs=[pl.BlockSpec((tm, d), lambda i, j: (i, 0)),
                  pl.BlockSpec((None, d, tn), lambda i, j: (layer, 0, j)),
                  pl.BlockSpec((None, d, tn), lambda i, j: (layer, 0, nj + j)),
                  pl.BlockSpec((CONV_WIDTH, tn), lambda i, j: (0, j)),
                  pl.BlockSpec((1, tn), lambda i, j: (0, j)),
                  bd, bd, bd, gw, gw, gw,
                  pl.BlockSpec((1, V7X_LANES), lambda i, j: (0, 0))],
        out_specs=[col, col, col, col, col,
                   pl.BlockSpec((tm, V7X_LANES), lambda i, j: (i, 0))],
        out_shape=[act, act, act, act, act, jax.ShapeDtypeStruct((t, V7X_LANES), F32)],
        scratch_shapes=[pltpu.VMEM((tm + V7X_SUBLANES, tn), F32),
                        pltpu.VMEM((nj, V7X_SUBLANES, tn), F32)],
        compiler_params=_params("arbitrary", "arbitrary"),
        name="ml_in",
    )(u, w_in, w_in, conv_w, conv_b.reshape(1, inner), bq, bk, bv, gq, gk, gv, bg)


def _log_sigmoid(x):
    return jnp.minimum(x, 0.0) - jnp.log1p(jnp.exp(-jnp.abs(x)))


def _gate_prep_kernel(gates_ref, cum_ref, gates_t_ref, cum_t_ref, *, rows, n_sub, keep):
    ri = lax.broadcasted_iota(jnp.int32, (rows, rows), 0)
    ci = lax.broadcasted_iota(jnp.int32, (rows, rows), 1)
    tri = (ri >= ci).astype(BF16)
    for c in range(n_sub):
        sl = slice(c * rows, (c + 1) * rows)
        gates = gates_ref[sl, :]
        log_f = _log_sigmoid(gates)
        hi = log_f.astype(BF16)
        r1 = log_f - hi.astype(F32)
        mid = r1.astype(BF16)
        lo = (r1 - mid.astype(F32)).astype(BF16)
        cum = _dot(tri, hi) + _dot(tri, mid) + _dot(tri, lo)
        cum_ref[sl, :] = cum
        gates_t_ref[:, sl] = gates.T[:keep, :]
        cum_t_ref[:, sl] = cum.T[:keep, :]


def _gate_prep(gates, *, seq):
    t, lanes = gates.shape
    step = _tile(seq, 1024)
    rows = _tile(step, CHUNK_ROWS)
    keep = V7X_SUBLANES
    kern = functools.partial(_gate_prep_kernel, rows=rows, n_sub=step // rows, keep=keep)
    return pl.pallas_call(
        kern,
        grid=(t // step,),
        in_specs=[pl.BlockSpec((step, lanes), lambda i: (i, 0))],
        out_specs=[pl.BlockSpec((step, lanes), lambda i: (i, 0)),
                   pl.BlockSpec((keep, step), lambda i: (0, i)),
                   pl.BlockSpec((keep, step), lambda i: (0, i))],
        out_shape=[jax.ShapeDtypeStruct((t, lanes), F32),
                   jax.ShapeDtypeStruct((keep, t), F32),
                   jax.ShapeDtypeStruct((keep, t), F32)],
        compiler_params=_params("parallel"),
        name="gate_prep",
    )(gates)


def _ml_chunk_kernel(q_ref, k_ref, v_ref, gates_ref, cum_ref, gates_t_ref, cum_t_ref, o_ref,
                     c_ref, cb_ref, n_ref, m_ref, *, rows, n_sub, heads, k_scale):
    h = pl.program_id(1)

    @pl.when(pl.program_id(2) == 0)
    def _():
        c_ref[...] = jnp.zeros_like(c_ref)
        cb_ref[...] = jnp.zeros_like(cb_ref)
        n_ref[...] = jnp.zeros_like(n_ref)
        m_ref[...] = jnp.zeros_like(m_ref)

    lanes = gates_ref.shape[1]
    ri = lax.broadcasted_iota(jnp.int32, (rows, rows), 0)
    ci = lax.broadcasted_iota(jnp.int32, (rows, rows), 1)
    causal = ri >= ci
    lane = lax.broadcasted_iota(jnp.int32, (1, lanes), 1)

    for c in range(n_sub):
        sl = slice(c * rows, (c + 1) * rows)
        ig = jnp.sum(jnp.where(lane == h, gates_ref[sl, :], 0.0), axis=1, keepdims=True)
        bc = jnp.sum(jnp.where(lane == heads + h, cum_ref[sl, :], 0.0), axis=1, keepdims=True)
        ig_row = gates_t_ref[pl.ds(h, 1), sl]
        bc_row = cum_t_ref[pl.ds(heads + h, 1), sl]

        m_prev = m_ref[...]
        log_d = jnp.where(causal, bc - bc_row + ig_row, -jnp.inf)
        m_inter = bc + m_prev
        m_t = jnp.maximum(jnp.max(log_d, axis=1, keepdims=True), m_inter)
        dmat = jnp.exp(log_d - m_t)
        inter = jnp.exp(m_inter - m_t)

        q = q_ref[sl, :]
        k = k_ref[sl, :]
        v = v_ref[sl, :]
        scores = _dot_nt(q, k) * (dmat * k_scale)
        n_vec = n_ref[...]
        num = _dot(scores.astype(BF16), v) + inter * _dot(q, cb_ref[...])
        qn = jnp.sum(q.astype(F32) * n_vec, axis=1, keepdims=True)
        den = jnp.sum(scores, axis=1, keepdims=True) + inter * qn
        o_ref[sl, :] = (num / jnp.maximum(jnp.abs(den), jnp.exp(-m_t))).astype(o_ref.dtype)

        b_last = bc[rows - 1:rows, :]
        log_w = b_last - bc + ig
        m_new = jnp.maximum(b_last + m_prev, jnp.max(log_w, axis=0, keepdims=True))
        wts = jnp.exp(log_w - m_new) * k_scale
        decay = jnp.exp(b_last + m_prev - m_new)
        kw = k.astype(F32) * wts
        c_new = decay * c_ref[...] + _dot_tn(kw.astype(BF16), v)
        c_ref[...] = c_new
        cb_ref[...] = c_new.astype(BF16)
        n_ref[...] = decay * n_vec + jnp.sum(kw, axis=0, keepdims=True)
        m_ref[...] = m_new


def _ml_chunk(q, k, v, gates, cum, gates_t, cum_t, *, batch, seq, heads):
    t, inner = q.shape
    dh = inner // heads
    step = _tile(seq, ML_STEP_ROWS)
    rows = _tile(step, CHUNK_ROWS)
    nsteps = seq // step
    blk = pl.BlockSpec((step, dh), lambda b, h, n: (b * nsteps + n, h))
    col = pl.BlockSpec((step, gates.shape[1]), lambda b, h, n: (b * nsteps + n, 0))
    row = pl.BlockSpec((gates_t.shape[0], step), lambda b, h, n: (0, b * nsteps + n))
    kern = functools.partial(_ml_chunk_kernel, rows=rows, n_sub=step // rows, heads=heads,
                             k_scale=float(dh) ** -0.5)
    return pl.pallas_call(
        kern,
        grid=(batch, heads, nsteps),
        in_specs=[blk, blk, blk, col, col, row, row],
        out_specs=blk,
        out_shape=jax.ShapeDtypeStruct((t, inner), BF16),
        scratch_shapes=[pltpu.VMEM((dh, dh), F32),
                        pltpu.VMEM((dh, dh), BF16),
                        pltpu.VMEM((1, dh), F32),
                        pltpu.VMEM((1, 1), F32)],
        compiler_params=_params("parallel", "parallel", "arbitrary"),
        name="ml_chunk",
    )(q, k, v, gates, cum, gates_t, cum_t)


def kernel(x, positions, norm_mix_g, norm_ffn_g, ret_w_in, ret_gn_g, ret_w_out, ml_w_in, ml_conv_w, ml_conv_b, ml_w_q, ml_w_k, ml_w_v, ml_w_gate, ml_b_gate, ml_gn_g, ml_skip, ml_w_out, ffn_w_gate, ffn_w_up, ffn_w_down, final_g):
    batch, seq, d_model = x.shape
    t = batch * seq
    depth = norm_mix_g.shape[0]
    n_mixers = 2

    qk_dim = d_model
    v_dim = ret_w_out.shape[1]
    dk = qk_dim // RET_HEADS
    dv = v_dim // RET_HEADS

    inv_freq = ROPE_BASE ** (-jnp.arange(dk // 2, dtype=F32) * (2.0 / dk))
    cos, sin = _rope_table(positions.reshape(t), inv_freq)
    log_gamma = jnp.log(1.0 - 2.0 ** (-5.0 - jnp.arange(RET_HEADS, dtype=F32)))

    ret_w_out_bf = ret_w_out.astype(BF16)
    ml_w_out_bf = ml_w_out.astype(BF16)
    ffn_w_down_bf = ffn_w_down.astype(BF16)

    h = x.reshape(t, d_model)
    u = _rmsnorm(h, norm_mix_g[0])
    out = None
    for i in range(depth):
        j = i // n_mixers
        if i % n_mixers == 0:
            qk = _ret_qk(u, ret_w_in, j, cos, sin, dk=dk, qk_dim=qk_dim)
            v = _matmul(u, ret_w_in, j, col_start=2 * qk_dim, col_count=v_dim, head_dim=dv,
                        name="ret_v")
            g = _matmul(u, ret_w_in, j, col_start=2 * qk_dim + v_dim, col_count=v_dim, name="ret_g")
            o = _ret_chunk(qk, v, log_gamma, batch=batch, seq=seq)
            h, u = _sublayer_out(_ret_out_kernel, [(o, 0), (g, 0)], [ret_gn_g[j]],
                                 ret_w_out_bf, j, h, norm_ffn_g[i], name="ret_out")
        else:
            q, k, v, xc, z, gates = _ml_in(u, ml_w_in, j, ml_conv_w[j], ml_conv_b[j],
                                           ml_w_q[j], ml_w_k[j], ml_w_v[j], ml_w_gate[j],
                                           ml_b_gate[j], seq=seq)
            cum, gates_t, cum_t = _gate_prep(gates, seq=seq)
            hid = _ml_chunk(q, k, v, gates, cum, gates_t, cum_t, batch=batch, seq=seq, heads=ML_HEADS)
            h, u = _sublayer_out(functools.partial(_ml_out_kernel, heads=ML_HEADS),
                                 [(hid, 0), (xc, 0), (z, 0)], [ml_gn_g[j], ml_skip[j]],
                                 ml_w_out_bf, j, h, norm_ffn_g[i], name="ml_out")
        mid = _ffn_up(u, ffn_w_gate, ffn_w_up, i)
        if i + 1 < depth:
            h, u = _sublayer_out(functools.partial(_ffn_down_kernel, final=False), [(mid, 0)], [],
                                 ffn_w_down_bf, i, h, norm_mix_g[i + 1], name="ffn_down")
        else:
            out = _sublayer_out(functools.partial(_ffn_down_kernel, final=True), [(mid, 0)], [],
                                ffn_w_down_bf, i, h, final_g, final=True, name="ffn_down_final")
    return out.reshape(batch, seq, d_model)
```

```python
import functools

import jax
import jax.numpy as jnp
from jax import lax
from jax.experimental import pallas as pl
from jax.experimental.pallas import tpu as pltpu

F32 = jnp.float32
BF16 = jnp.bfloat16

EPS = 1e-6
ROPE_BASE = 10000.0
RET_HEADS = 8
ML_HEADS = 4
CONV_WIDTH = 4

V7X_VMEM_BYTES = 64 * 1024 * 1024
V7X_LANES = 128
V7X_SUBLANES = 8
V7X_MXU_DIM = 256

CHUNK_ROWS = V7X_MXU_DIM
ML_STEP_ROWS = 1024
STREAM_ROWS = 1024
MATMUL_ROWS = 2048
RESIDENT_ROWS = 256
ML_OUT_GROUPS = 4
ML_IN_ROW_BLOCKS = 4
VMEM_LIMIT = V7X_VMEM_BYTES - 8 * 1024 * 1024


def _tile(n, pref):
    t = min(pref, n)
    while n % t:
        t //= 2
    return t


def _params(*semantics):
    return pltpu.CompilerParams(dimension_semantics=semantics, vmem_limit_bytes=VMEM_LIMIT)


def _sigmoid(x):
    return 1.0 / (1.0 + jnp.exp(-x))


def _dot(a, b):
    return jnp.dot(a, b, preferred_element_type=F32)


def _dot_nt(a, b):
    return lax.dot_general(a, b, (((1,), (1,)), ((), ())), preferred_element_type=F32)


def _dot_tn(a, b):
    return lax.dot_general(a, b, (((0,), (0,)), ((), ())), preferred_element_type=F32)


def _rms(x, g):
    ms = jnp.mean(x * x, axis=-1, keepdims=True)
    return (x * g) * lax.rsqrt(ms + EPS)


def _head_norm(o, g):
    mu = jnp.mean(o, axis=-1, keepdims=True)
    d = o - mu
    var = jnp.mean(d * d, axis=-1, keepdims=True)
    return d * lax.rsqrt(var + EPS) * g


def _rmsnorm_kernel(x_ref, g_ref, o_ref):
    o_ref[...] = _rms(x_ref[...], g_ref[...]).astype(o_ref.dtype)


def _rmsnorm(x, g):
    t, d = x.shape
    tm = _tile(t, 512)
    return pl.pallas_call(
        _rmsnorm_kernel,
        grid=(t // tm,),
        in_specs=[pl.BlockSpec((tm, d), lambda i: (i, 0)),
                  pl.BlockSpec((1, d), lambda i: (0, 0))],
        out_specs=pl.BlockSpec((tm, d), lambda i: (i, 0)),
        out_shape=jax.ShapeDtypeStruct((t, d), BF16),
        compiler_params=_params("parallel"),
        name="rmsnorm",
    )(x, g.reshape(1, d))


def _rope_kernel(pos_ref, invf_ref, cos_ref, sin_ref):
    ang = pos_ref[...].astype(F32) * invf_ref[...]
    cos_ref[...] = jnp.cos(ang)
    sin_ref[...] = jnp.sin(ang)


def _rope_table(pos, inv_freq):
    t = pos.shape[0]
    half = inv_freq.shape[0]
    tm = _tile(t, 2048)
    return pl.pallas_call(
        _rope_kernel,
        grid=(t // tm,),
        in_specs=[pl.BlockSpec((tm, 1), lambda i: (i, 0)),
                  pl.BlockSpec((1, half), lambda i: (0, 0))],
        out_specs=[pl.BlockSpec((tm, half), lambda i: (i, 0))] * 2,
        out_shape=[jax.ShapeDtypeStruct((t, half), F32)] * 2,
        compiler_params=_params("parallel"),
        name="rope_table",
    )(pos.reshape(t, 1), inv_freq.reshape(1, half))


def _cast_weights_once(pairs):
    @pl.when(pl.program_id(1) == 0)
    def _():
        for w_ref, wb_ref in pairs:
            wb_ref[...] = w_ref[...].astype(BF16)


def _store_heads(o_ref, acc):
    hd = o_ref.shape[2]
    for hh in range(o_ref.shape[0]):
        o_ref[hh] = acc[:, hh * hd:(hh + 1) * hd].astype(o_ref.dtype)


def _matmul_kernel(u_ref, w_ref, o_ref, wb_ref, *, head_major):
    _cast_weights_once([(w_ref, wb_ref)])
    acc = _dot(u_ref[...], wb_ref[...])
    if head_major:
        _store_heads(o_ref, acc)
    else:
        o_ref[...] = acc.astype(o_ref.dtype)


def _matmul(u, w, layer, *, col_start, col_count, head_dim=None, name):
    t, d = u.shape
    tm = _tile(t, MATMUL_ROWS)
    tn = _tile(col_count, 1024)
    assert col_start % tn == 0
    j0 = col_start // tn
    if head_dim is None:
        out_spec = pl.BlockSpec((tm, tn), lambda j, i: (i, j))
        out_shape = jax.ShapeDtypeStruct((t, col_count), BF16)
    else:
        out_spec = pl.BlockSpec((tn // head_dim, tm, head_dim), lambda j, i: (j, i, 0))
        out_shape = jax.ShapeDtypeStruct((col_count // head_dim, t, head_dim), BF16)
    return pl.pallas_call(
        functools.partial(_matmul_kernel, head_major=head_dim is not None),
        grid=(col_count // tn, t // tm),
        in_specs=[pl.BlockSpec((tm, d), lambda j, i: (i, 0)),
                  pl.BlockSpec((None, d, tn), lambda j, i: (layer, 0, j0 + j))],
        out_specs=out_spec,
        out_shape=out_shape,
        scratch_shapes=[pltpu.VMEM((d, tn), BF16)],
        compiler_params=_params("arbitrary", "arbitrary"),
        name=name,
    )(u, w)


def _ret_qk_kernel(u_ref, w_ref, cos_ref, sin_ref, o_ref, wb_ref, *, tn, dk, n_q_blocks, k_scale):
    _cast_weights_once([(w_ref, wb_ref)])
    acc = _dot(u_ref[...], wb_ref[...])
    scale = jnp.where(pl.program_id(0) >= n_q_blocks, k_scale, 1.0).astype(F32)
    c = cos_ref[...] * scale
    s = sin_ref[...] * scale
    half = dk // 2
    for hh in range(tn // dk):
        t1 = acc[:, hh * dk:hh * dk + half]
        t2 = acc[:, hh * dk + half:(hh + 1) * dk]
        o_ref[hh, :, :half] = (t1 * c - t2 * s).astype(o_ref.dtype)
        o_ref[hh, :, half:] = (t1 * s + t2 * c).astype(o_ref.dtype)


def _ret_qk(u, w, layer, cos, sin, *, dk, qk_dim):
    t, d = u.shape
    tm = _tile(t, STREAM_ROWS)
    tn = _tile(qk_dim, 1024)
    kern = functools.partial(_ret_qk_kernel, tn=tn, dk=dk, n_q_blocks=qk_dim // tn,
                             k_scale=float(dk) ** -0.5)
    return pl.pallas_call(
        kern,
        grid=(2 * qk_dim // tn, t // tm),
        in_specs=[pl.BlockSpec((tm, d), lambda j, i: (i, 0)),
                  pl.BlockSpec((None, d, tn), lambda j, i: (layer, 0, j)),
                  pl.BlockSpec((tm, dk // 2), lambda j, i: (i, 0)),
                  pl.BlockSpec((tm, dk // 2), lambda j, i: (i, 0))],
        out_specs=pl.BlockSpec((tn // dk, tm, dk), lambda j, i: (j, i, 0)),
        out_shape=jax.ShapeDtypeStruct((2 * qk_dim // dk, t, dk), BF16),
        scratch_shapes=[pltpu.VMEM((d, tn), BF16)],
        compiler_params=_params("arbitrary", "arbitrary"),
        name="ret_qk",
    )(u, w, cos, sin)


def _ret_mix_kernel(lg_ref, qk_ref, v_ref, g_ref, gn_ref, w_ref, h_ref, ng_ref, hout_ref, u_ref,
                    state_ref, dmat_ref, xi_ref, zeta_ref, o_ref, *, seq):
    i = pl.program_id(0)
    heads, rows, dv = v_ref.shape

    @pl.when(i == 0)
    def _():
        ri = lax.broadcasted_iota(jnp.int32, (rows, rows), 0)
        ci = lax.broadcasted_iota(jnp.int32, (rows, rows), 1)
        rel = (ri - ci).astype(F32)
        idx = lax.broadcasted_iota(jnp.int32, (rows, 1), 0).astype(F32)
        for hh in range(heads):
            lg = lg_ref[hh]
            dmat_ref[hh] = jnp.where(rel >= 0.0, jnp.exp(jnp.maximum(rel, 0.0) * lg), 0.0)
            xi_ref[hh] = jnp.exp((idx + 1.0) * lg)
            zeta_ref[hh] = jnp.exp((rows - 1.0 - idx) * lg)

    @pl.when((i * rows) % seq == 0)
    def _():
        state_ref[...] = jnp.zeros_like(state_ref)

    for hh in range(heads):
        q = qk_ref[hh]
        k = qk_ref[heads + hh]
        v = v_ref[hh]
        state = state_ref[hh]
        chunk_decay = jnp.exp(jnp.full((1, 1), float(rows), F32) * lg_ref[hh])
        scores = _dot_nt(q, k) * dmat_ref[hh]
        o_ref[hh] = _dot(scores.astype(BF16), v) + _dot(q, state.astype(BF16)) * xi_ref[hh]
        kz = (k.astype(F32) * zeta_ref[hh]).astype(BF16)
        state_ref[hh] = chunk_decay * state + _dot_tn(kz, v)

    def build_group(hh):
        sl = slice(hh * dv, (hh + 1) * dv)
        g = g_ref[:, sl].astype(F32)
        return (_head_norm(o_ref[hh], gn_ref[:, sl]) * (g * _sigmoid(g))).astype(BF16)

    _project_head_groups(build_group, heads, w_ref, h_ref, ng_ref, hout_ref, u_ref)


def _ret_mix(qk, v, g, gn_g, log_gamma, w, layer, h, ng, *, seq):
    heads, t, dv = v.shape
    dk = qk.shape[2]
    d = h.shape[1]
    k_dim = heads * dv
    rows = _tile(seq, CHUNK_ROWS)
    row_spec = pl.BlockSpec((rows, d), lambda i: (i, 0))
    return pl.pallas_call(
        functools.partial(_ret_mix_kernel, seq=seq),
        grid=(t // rows,),
        in_specs=[pl.BlockSpec(memory_space=pltpu.SMEM),
                  pl.BlockSpec((2 * heads, rows, dk), lambda i: (0, i, 0)),
                  pl.BlockSpec((heads, rows, dv), lambda i: (0, i, 0)),
                  pl.BlockSpec((rows, k_dim), lambda i: (i, 0)),
                  pl.BlockSpec((1, k_dim), lambda i: (0, 0)),
                  pl.BlockSpec((None, k_dim, d), lambda i: (layer, 0, 0), pipeline_mode=pl.Buffered(1)),
                  row_spec,
                  pl.BlockSpec((1, d), lambda i: (0, 0))],
        out_specs=[row_spec, row_spec],
        out_shape=[jax.ShapeDtypeStruct((t, d), F32), jax.ShapeDtypeStruct((t, d), BF16)],
        scratch_shapes=[pltpu.VMEM((heads, dk, dv), F32),
                        pltpu.VMEM((heads, rows, rows), F32),
                        pltpu.VMEM((heads, rows, 1), F32),
                        pltpu.VMEM((heads, rows, 1), F32),
                        pltpu.VMEM((heads, rows, dv), F32)],
        compiler_params=_params("arbitrary"),
        name="ret_mix",
    )(log_gamma, qk, v, g, gn_g.reshape(1, k_dim), w, h, ng.reshape(1, d))


def _residual_and_norm(y, w_ref, h_ref, ng_ref, out_refs, final):
    hn = h_ref[...] + _dot(y, w_ref[...])
    if final:
        out_refs[0][...] = _rms(hn, ng_ref[...])
    else:
        out_refs[0][...] = hn
        out_refs[1][...] = _rms(hn, ng_ref[...]).astype(out_refs[1].dtype)


def _ffn_down_kernel(y_ref, w_ref, h_ref, ng_ref, *out_refs, final):
    _residual_and_norm(y_ref[...], w_ref, h_ref, ng_ref, out_refs, final)


def _project_head_groups(build_group, groups, w_ref, h_ref, ng_ref, hout_ref, u_ref):
    gk = w_ref.shape[0] // groups
    acc = h_ref[...]
    for c in range(groups):
        acc = acc + _dot(build_group(c), w_ref[c * gk:(c + 1) * gk, :])
    hout_ref[...] = acc
    u_ref[...] = _rms(acc, ng_ref[...]).astype(u_ref.dtype)


def _ml_out_kernel(hid_ref, xc_ref, z_ref, gn_ref, skip_ref, w_ref, h_ref, ng_ref, hout_ref, u_ref,
                   *, heads):
    dh = hid_ref.shape[1] // heads
    per = heads // ML_OUT_GROUPS

    def build_group(c):
        ys = []
        for hh in range(c * per, (c + 1) * per):
            sl = slice(hh * dh, (hh + 1) * dh)
            z = z_ref[:, sl].astype(F32)
            hn = _head_norm(hid_ref[:, sl].astype(F32), gn_ref[:, sl])
            y = (hn + skip_ref[:, sl] * xc_ref[:, sl].astype(F32)) * (z * _sigmoid(z))
            ys.append(y.astype(BF16))
        return jnp.concatenate(ys, axis=-1)

    _project_head_groups(build_group, ML_OUT_GROUPS, w_ref, h_ref, ng_ref, hout_ref, u_ref)


def _sublayer_out(kern, acts, vecs, w, layer, h, ng, *, final=False, name):
    t, d = h.shape
    k = w.shape[1]
    tm = _tile(t, RESIDENT_ROWS)
    row_spec = pl.BlockSpec((tm, d), lambda i: (i, 0))
    vec_spec = pl.BlockSpec((1, k), lambda i: (0, 0))
    if final:
        out_specs, out_shape = row_spec, jax.ShapeDtypeStruct((t, d), F32)
    else:
        out_specs = [row_spec, row_spec]
        out_shape = [jax.ShapeDtypeStruct((t, d), F32), jax.ShapeDtypeStruct((t, d), BF16)]
    act_specs = [pl.BlockSpec((tm, k), functools.partial(lambda i, cb: (i, cb), cb=cb)) if a.ndim == 2
                 else pl.BlockSpec((a.shape[0], tm, a.shape[2]), lambda i: (0, i, 0))
                 for a, cb in acts]
    return pl.pallas_call(
        kern,
        grid=(t // tm,),
        in_specs=act_specs + [vec_spec] * len(vecs) + [
            pl.BlockSpec((None, k, d), lambda i: (layer, 0, 0), pipeline_mode=pl.Buffered(1)),
            row_spec,
            pl.BlockSpec((1, d), lambda i: (0, 0))],
        out_specs=out_specs,
        out_shape=out_shape,
        compiler_params=_params("parallel"),
        name=name,
    )(*[a for a, _ in acts], *[v.reshape(1, k) for v in vecs], w, h, ng.reshape(1, d))


def _ffn_up_kernel(u_ref, wg_ref, wu_ref, o_ref, wgb_ref, wub_ref):
    _cast_weights_once([(wg_ref, wgb_ref), (wu_ref, wub_ref)])
    u = u_ref[...]
    a = _dot(u, wgb_ref[...])
    b = _dot(u, wub_ref[...])
    o_ref[...] = (a * _sigmoid(a) * b).astype(o_ref.dtype)


def _ffn_up(u, wg, wu, layer):
    t, d = u.shape
    f = wg.shape[2]
    tm = _tile(t, STREAM_ROWS)
    tf = _tile(f, 512)
    return pl.pallas_call(
        _ffn_up_kernel,
        grid=(f // tf, t // tm),
        in_specs=[pl.BlockSpec((tm, d), lambda j, i: (i, 0)),
                  pl.BlockSpec((None, d, tf), lambda j, i: (layer, 0, j)),
                  pl.BlockSpec((None, d, tf), lambda j, i: (layer, 0, j))],
        out_specs=pl.BlockSpec((tm, tf), lambda j, i: (i, j)),
        out_shape=jax.ShapeDtypeStruct((t, f), BF16),
        scratch_shapes=[pltpu.VMEM((d, tf), BF16), pltpu.VMEM((d, tf), BF16)],
        compiler_params=_params("arbitrary", "arbitrary"),
        name="ffn_up",
    )(u, wg, wu)


def _ml_in_kernel(u_ref, wx_ref, wz_ref, cw_ref, cb_ref, bq_ref, bk_ref, bv_ref,
                  gq_ref, gk_ref, gv_ref, bg_ref,
                  q_ref, k_ref, v_ref, xc_ref, z_ref, gates_ref,
                  ext_ref, halo_ref, *, tm, tn, seq):
    i = pl.program_id(0)
    j = pl.program_id(1)
    hs = V7X_SUBLANES

    @pl.when(i == 0)
    def _():
        halo_ref[j] = jnp.zeros(halo_ref.shape[1:], F32)

    seq_start = (i * tm) % seq == 0
    ext_ref[0:hs, :] = jnp.where(seq_start, 0.0, halo_ref[j])
    rb = tm // ML_IN_ROW_BLOCKS
    wx = wx_ref[...].astype(BF16)
    wz = wz_ref[...].astype(BF16)
    for r in range(ML_IN_ROW_BLOCKS):
        rows = slice(r * rb, (r + 1) * rb)
        u = u_ref[rows, :]
        xm = _dot(u, wx)
        ext_ref[hs + r * rb:hs + (r + 1) * rb, :] = xm
        z_ref[rows, :] = _dot(u, wz).astype(z_ref.dtype)
        conv = cb_ref[...]
        for tap in range(CONV_WIDTH):
            off = hs - (CONV_WIDTH - 1) + tap + r * rb
            conv = conv + ext_ref[off:off + rb, :] * cw_ref[tap:tap + 1, :]
        xc = conv * _sigmoid(conv)
        xc_ref[rows, :] = xc.astype(xc_ref.dtype)
        gacc = jnp.where(j == 0, bg_ref[...], gates_ref[rows, :])
        for c in range(tn // V7X_MXU_DIM):
            sl = slice(c * V7X_MXU_DIM, (c + 1) * V7X_MXU_DIM)
            xcb = xc[:, sl].astype(BF16)
            xmb = xm[:, sl].astype(BF16)
            q = _dot(xcb, bq_ref[c]).astype(BF16)
            k = _dot(xcb, bk_ref[c]).astype(BF16)
            v = _dot(xmb, bv_ref[c]).astype(BF16)
            q_ref[rows, sl] = q
            k_ref[rows, sl] = k
            v_ref[rows, sl] = v
            gacc = gacc + _dot(q, gq_ref[sl, :]) + _dot(k, gk_ref[sl, :]) + _dot(v, gv_ref[sl, :])
        gates_ref[rows, :] = gacc
    halo_ref[j] = ext_ref[tm:tm + hs, :]


def _block_diag_tiles(w, tile):
    nb, bo, bi = w.shape
    per = tile // bi
    rows = jnp.swapaxes(w, 1, 2).reshape(nb // per, tile, bo)
    tiled = jnp.tile(rows, (1, 1, per))
    r_blk = lax.broadcasted_iota(jnp.int32, (tile, tile), 0) // bi
    c_blk = lax.broadcasted_iota(jnp.int32, (tile, tile), 1) // bo
    return jnp.where(r_blk == c_blk, tiled, 0.0)


def _ml_in(u, w_in, layer, conv_w, conv_b, w_q, w_k, w_v, w_gate, b_gate, *, seq):
    t, d = u.shape
    inner = w_in.shape[2] // 2
    tm = _tile(seq, 1024)
    tn = 512
    nj = inner // tn
    per_tile = tn // V7X_MXU_DIM
    ngates = w_gate.shape[1]

    def pad_gate(wg):
        return jnp.pad(wg, ((0, 0), (0, V7X_LANES - ngates))).astype(BF16)

    gq, gk, gv = (pad_gate(w_gate[s * inner:(s + 1) * inner]) for s in range(3))
    bg = jnp.pad(b_gate, (0, V7X_LANES - ngates)).reshape(1, V7X_LANES).astype(F32)
    bq, bk, bv = (_block_diag_tiles(w, V7X_MXU_DIM).astype(BF16) for w in (w_q, w_k, w_v))

    col = pl.BlockSpec((tm, tn), lambda i, j: (i, j))
    bd = pl.BlockSpec((per_tile, V7X_MXU_DIM, V7X_MXU_DIM), lambda i, j: (j, 0, 0))
    gw = pl.BlockSpec((tn, V7X_LANES), lambda i, j: (j, 0))
    act = jax.ShapeDtypeStruct((t, inner), BF16)
    kern = functools.partial(_ml_in_kernel, tm=tm, tn=tn, seq=seq)
    return pl.pallas_call(
        kern,
        grid=(t // tm, nj),
        in_specs=[pl.BlockSpec((tm, d), lambda i, j: (i, 0)),
                  pl.BlockSpec((None, d, tn), lambda i, j: (layer, 0, j)),
                  pl.BlockSpec((None, d, tn), lambda i, j: (layer, 0, nj + j)),
                  pl.BlockSpec((CONV_WIDTH, tn), lambda i, j: (0, j)),
                  pl.BlockSpec((1, tn), lambda i, j: (0, j)),
                  bd, bd, bd, gw, gw, gw,
                  pl.BlockSpec((1, V7X_LANES), lambda i, j: (0, 0))],
        out_specs=[col, col, col, col, col,
                   pl.BlockSpec((tm, V7X_LANES), lambda i, j: (i, 0))],
        out_shape=[act, act, act, act, act, jax.ShapeDtypeStruct((t, V7X_LANES), F32)],
        scratch_shapes=[pltpu.VMEM((tm + V7X_SUBLANES, tn), F32),
                        pltpu.VMEM((nj, V7X_SUBLANES, tn), F32)],
        compiler_params=_params("arbitrary", "arbitrary"),
        name="ml_in",
    )(u, w_in, w_in, conv_w, conv_b.reshape(1, inner), bq, bk, bv, gq, gk, gv, bg)


def _log_sigmoid(x):
    return jnp.minimum(x, 0.0) - jnp.log1p(jnp.exp(-jnp.abs(x)))


def _gate_prep_kernel(gates_ref, cum_ref, gates_t_ref, cum_t_ref, *, rows, n_sub, keep):
    ri = lax.broadcasted_iota(jnp.int32, (rows, rows), 0)
    ci = lax.broadcasted_iota(jnp.int32, (rows, rows), 1)
    tri = (ri >= ci).astype(BF16)
    for c in range(n_sub):
        sl = slice(c * rows, (c + 1) * rows)
        gates = gates_ref[sl, :]
        log_f = _log_sigmoid(gates)
        hi = log_f.astype(BF16)
        r1 = log_f - hi.astype(F32)
        mid = r1.astype(BF16)
        lo = (r1 - mid.astype(F32)).astype(BF16)
        cum = _dot(tri, hi) + _dot(tri, mid) + _dot(tri, lo)
        cum_ref[sl, :] = cum
        gates_t_ref[:, sl] = gates.T[:keep, :]
        cum_t_ref[:, sl] = cum.T[:keep, :]


def _gate_prep(gates, *, seq):
    t, lanes = gates.shape
    step = _tile(seq, 1024)
    rows = _tile(step, CHUNK_ROWS)
    keep = V7X_SUBLANES
    kern = functools.partial(_gate_prep_kernel, rows=rows, n_sub=step // rows, keep=keep)
    return pl.pallas_call(
        kern,
        grid=(t // step,),
        in_specs=[pl.BlockSpec((step, lanes), lambda i: (i, 0))],
        out_specs=[pl.BlockSpec((step, lanes), lambda i: (i, 0)),
                   pl.BlockSpec((keep, step), lambda i: (0, i)),
                   pl.BlockSpec((keep, step), lambda i: (0, i))],
        out_shape=[jax.ShapeDtypeStruct((t, lanes), F32),
                   jax.ShapeDtypeStruct((keep, t), F32),
                   jax.ShapeDtypeStruct((keep, t), F32)],
        compiler_params=_params("parallel"),
        name="gate_prep",
    )(gates)


def _ml_chunk_kernel(q_ref, k_ref, v_ref, gates_ref, cum_ref, gates_t_ref, cum_t_ref, o_ref,
                     c_ref, cb_ref, n_ref, m_ref, *, rows, n_sub, heads, k_scale):
    h = pl.program_id(1)

    @pl.when(pl.program_id(2) == 0)
    def _():
        c_ref[...] = jnp.zeros_like(c_ref)
        cb_ref[...] = jnp.zeros_like(cb_ref)
        n_ref[...] = jnp.zeros_like(n_ref)
        m_ref[...] = jnp.zeros_like(m_ref)

    lanes = gates_ref.shape[1]
    ri = lax.broadcasted_iota(jnp.int32, (rows, rows), 0)
    ci = lax.broadcasted_iota(jnp.int32, (rows, rows), 1)
    causal = ri >= ci
    lane = lax.broadcasted_iota(jnp.int32, (1, lanes), 1)

    for c in range(n_sub):
        sl = slice(c * rows, (c + 1) * rows)
        ig = jnp.sum(jnp.where(lane == h, gates_ref[sl, :], 0.0), axis=1, keepdims=True)
        bc = jnp.sum(jnp.where(lane == heads + h, cum_ref[sl, :], 0.0), axis=1, keepdims=True)
        ig_row = gates_t_ref[pl.ds(h, 1), sl]
        bc_row = cum_t_ref[pl.ds(heads + h, 1), sl]

        m_prev = m_ref[...]
        log_d = jnp.where(causal, bc - bc_row + ig_row, -jnp.inf)
        m_inter = bc + m_prev
        m_t = jnp.maximum(jnp.max(log_d, axis=1, keepdims=True), m_inter)
        dmat = jnp.exp(log_d - m_t)
        inter = jnp.exp(m_inter - m_t)

        q = q_ref[sl, :]
        k = k_ref[sl, :]
        v = v_ref[sl, :]
        scores = _dot_nt(q, k) * (dmat * k_scale)
        n_vec = n_ref[...]
        num = _dot(scores.astype(BF16), v) + inter * _dot(q, cb_ref[...])
        qn = jnp.sum(q.astype(F32) * n_vec, axis=1, keepdims=True)
        den = jnp.sum(scores, axis=1, keepdims=True) + inter * qn
        o_ref[sl, :] = (num / jnp.maximum(jnp.abs(den), jnp.exp(-m_t))).astype(o_ref.dtype)

        b_last = bc[rows - 1:rows, :]
        log_w = b_last - bc + ig
        m_new = jnp.maximum(b_last + m_prev, jnp.max(log_w, axis=0, keepdims=True))
        wts = jnp.exp(log_w - m_new) * k_scale
        decay = jnp.exp(b_last + m_prev - m_new)
        kw = k.astype(F32) * wts
        c_new = decay * c_ref[...] + _dot_tn(kw.astype(BF16), v)
        c_ref[...] = c_new
        cb_ref[...] = c_new.astype(BF16)
        n_ref[...] = decay * n_vec + jnp.sum(kw, axis=0, keepdims=True)
        m_ref[...] = m_new


def _ml_chunk(q, k, v, gates, cum, gates_t, cum_t, *, batch, seq, heads):
    t, inner = q.shape
    dh = inner // heads
    step = _tile(seq, ML_STEP_ROWS)
    rows = _tile(step, CHUNK_ROWS)
    nsteps = seq // step
    blk = pl.BlockSpec((step, dh), lambda b, h, n: (b * nsteps + n, h))
    col = pl.BlockSpec((step, gates.shape[1]), lambda b, h, n: (b * nsteps + n, 0))
    row = pl.BlockSpec((gates_t.shape[0], step), lambda b, h, n: (0, b * nsteps + n))
    kern = functools.partial(_ml_chunk_kernel, rows=rows, n_sub=step // rows, heads=heads,
                             k_scale=float(dh) ** -0.5)
    return pl.pallas_call(
        kern,
        grid=(batch, heads, nsteps),
        in_specs=[blk, blk, blk, col, col, row, row],
        out_specs=blk,
        out_shape=jax.ShapeDtypeStruct((t, inner), BF16),
        scratch_shapes=[pltpu.VMEM((dh, dh), F32),
                        pltpu.VMEM((dh, dh), BF16),
                        pltpu.VMEM((1, dh), F32),
                        pltpu.VMEM((1, 1), F32)],
        compiler_params=_params("parallel", "parallel", "arbitrary"),
        name="ml_chunk",
    )(q, k, v, gates, cum, gates_t, cum_t)


def kernel(x, positions, norm_mix_g, norm_ffn_g, ret_w_in, ret_gn_g, ret_w_out, ml_w_in, ml_conv_w, ml_conv_b, ml_w_q, ml_w_k, ml_w_v, ml_w_gate, ml_b_gate, ml_gn_g, ml_skip, ml_w_out, ffn_w_gate, ffn_w_up, ffn_w_down, final_g):
    batch, seq, d_model = x.shape
    t = batch * seq
    depth = norm_mix_g.shape[0]
    n_mixers = 2

    qk_dim = d_model
    v_dim = ret_w_out.shape[1]
    dk = qk_dim // RET_HEADS
    dv = v_dim // RET_HEADS

    inv_freq = ROPE_BASE ** (-jnp.arange(dk // 2, dtype=F32) * (2.0 / dk))
    cos, sin = _rope_table(positions.reshape(t), inv_freq)
    log_gamma = jnp.log(1.0 - 2.0 ** (-5.0 - jnp.arange(RET_HEADS, dtype=F32)))

    ret_w_out_bf = ret_w_out.astype(BF16)
    ml_w_out_bf = ml_w_out.astype(BF16)
    ffn_w_down_bf = ffn_w_down.astype(BF16)

    h = x.reshape(t, d_model)
    u = _rmsnorm(h, norm_mix_g[0])
    out = None
    for i in range(depth):
        j = i // n_mixers
        if i % n_mixers == 0:
            qk = _ret_qk(u, ret_w_in, j, cos, sin, dk=dk, qk_dim=qk_dim)
            v = _matmul(u, ret_w_in, j, col_start=2 * qk_dim, col_count=v_dim, head_dim=dv,
                        name="ret_v")
            g = _matmul(u, ret_w_in, j, col_start=2 * qk_dim + v_dim, col_count=v_dim, name="ret_g")
            h, u = _ret_mix(qk, v, g, ret_gn_g[j], log_gamma, ret_w_out_bf, j, h, norm_ffn_g[i],
                            seq=seq)
        else:
            q, k, v, xc, z, gates = _ml_in(u, ml_w_in, j, ml_conv_w[j], ml_conv_b[j],
                                           ml_w_q[j], ml_w_k[j], ml_w_v[j], ml_w_gate[j],
                                           ml_b_gate[j], seq=seq)
            cum, gates_t, cum_t = _gate_prep(gates, seq=seq)
            hid = _ml_chunk(q, k, v, gates, cum, gates_t, cum_t, batch=batch, seq=seq, heads=ML_HEADS)
            h, u = _sublayer_out(functools.partial(_ml_out_kernel, heads=ML_HEADS),
                                 [(hid, 0), (xc, 0), (z, 0)], [ml_gn_g[j], ml_skip[j]],
                                 ml_w_out_bf, j, h, norm_ffn_g[i], name="ml_out")
        mid = _ffn_up(u, ffn_w_gate, ffn_w_up, i)
        if i + 1 < depth:
            h, u = _sublayer_out(functools.partial(_ffn_down_kernel, final=False), [(mid, 0)], [],
                                 ffn_w_down_bf, i, h, norm_mix_g[i + 1], name="ffn_down")
        else:
            out = _sublayer_out(functools.partial(_ffn_down_kernel, final=True), [(mid, 0)], [],
                                ffn_w_down_bf, i, h, final_g, final=True, name="ffn_down_final")
    return out.reshape(batch, seq, d_model)
```
